```python
import math
import jax
import jax.numpy as jnp
from jax import lax
import numpy as np

D_MODEL = 2048
BATCH = 16
SEQ = 256
DEPTH = 2
DEC_BATCH = 8
DEC_SEQ = 2048
PAST_LEN = 512

GRID_W = 64
POS_BASE = 10000.0
EPS = 1e-6
GLA_HEADS = 4
GLA_DK = D_MODEL // 2 // GLA_HEADS
GLA_DV = D_MODEL // GLA_HEADS
GLA_QK = GLA_HEADS * GLA_DK
GLA_V = GLA_HEADS * GLA_DV
GLA_LOWRANK = 16
GLA_TAU = 16.0
GLA_CHUNK = 64
D_INNER = 2 * D_MODEL
SSD_P = 64
SSD_HEADS = D_INNER // SSD_P
SSD_N = 128
SSD_GROUPS = 8
SSD_HPG = SSD_HEADS // SSD_GROUPS
SSD_BC = SSD_GROUPS * SSD_N
CONV_DIM = D_INNER + 2 * SSD_BC
D_CONV = 5
CONV_PAD = D_CONV // 2
SSD_CHUNK = 128
DT_MIN = 0.001
DT_MAX = 0.1
IN_SIZES = (GLA_QK, GLA_QK, GLA_V, GLA_V, 2 * GLA_LOWRANK, D_INNER, CONV_DIM, 2 * SSD_HEADS, D_MODEL, D_MODEL)
IN_SPLITS = tuple(int(s) for s in np.cumsum(IN_SIZES)[:-1])
D_IN_PROJ = sum(IN_SIZES)
N_EXPERTS = 16
N_EXPERT_GROUPS = 4
EXPERTS_PER_GROUP = N_EXPERTS // N_EXPERT_GROUPS
TOP_K = 2
D_EXPERT = D_MODEL // 2
MOE_BLOCK = 256

kernel_name = 'bidir_gla_ssd_moe_diffusion_step'


def rms_norm(x, gain):
    xf = x.astype(jnp.float32)
    y = xf * lax.rsqrt(jnp.mean(xf * xf, axis=-1, keepdims=True) + EPS)
    return (y * gain.astype(jnp.float32)).astype(x.dtype)


def sincos_2d(rows, cols, dim):
    quarter = dim // 4
    omega = 1.0 / (POS_BASE ** (jnp.arange(quarter, dtype=jnp.float32) / quarter))
    def axis_embed(n):
        ang = jnp.arange(n, dtype=jnp.float32)[:, None] * omega[None, :]
        return jnp.concatenate([jnp.sin(ang), jnp.cos(ang)], axis=-1)
    er = jnp.broadcast_to(axis_embed(rows)[:, None, :], (rows, cols, dim // 2))
    ec = jnp.broadcast_to(axis_embed(cols)[None, :, :], (rows, cols, dim // 2))
    return jnp.concatenate([er, ec], axis=-1).reshape(rows * cols, dim)


def gla_chunk_scan(q, k, v, log_a, s0):
    bsz, t_len = q.shape[:2]
    nc = t_len // GLA_CHUNK
    chunks = lambda t: jnp.moveaxis(t.reshape(bsz, nc, GLA_CHUNK, *t.shape[2:]), 1, 0)
    causal = jnp.tril(jnp.ones((GLA_CHUNK, GLA_CHUNK), dtype=bool))
    def step(s, inp):
        qc, kc, vc, ac = inp
        b = jnp.cumsum(ac, axis=1)
        b_last = b[:, -1]
        qe = qc * jnp.exp(b)
        ke = kc * jnp.exp(-b)
        att = jnp.where(causal, jnp.einsum('bthd,bshd->bhts', qe, ke), 0.0)
        o = jnp.einsum('bhts,bshv->bthv', att, vc) + jnp.einsum('bthd,bhdv->bthv', qe, s)
        kd = kc * jnp.exp(b_last[:, None] - b)
        s_new = jnp.exp(b_last)[..., None] * s + jnp.einsum('bshd,bshv->bhdv', kd, vc)
        return s_new, o
    s_fin, oc = lax.scan(step, s0, (chunks(q), chunks(k), chunks(v), chunks(log_a)))
    return jnp.moveaxis(oc, 0, 1).reshape(bsz, t_len, GLA_HEADS, GLA_DV), s_fin


def ssd_chunk_scan(x, dt, a, bm, cm, h0):
    bsz, t_len = x.shape[:2]
    nc = t_len // SSD_CHUNK
    chunks = lambda t: jnp.moveaxis(t.reshape(bsz, nc, SSD_CHUNK, *t.shape[2:]), 1, 0)
    causal = jnp.tril(jnp.ones((SSD_CHUNK, SSD_CHUNK), dtype=bool))
    def step(h, inp):
        xc, dtc, bc, cc = inp
        cum = jnp.cumsum(dtc * a, axis=1)
        cum_t = jnp.moveaxis(cum, 1, -1)
        seg = cum_t[..., :, None] - cum_t[..., None, :]
        decay = jnp.exp(jnp.where(causal, seg, -jnp.inf))
        cb = jnp.einsum('btgn,bsgn->bgts', cc, bc)
        xdt = xc * dtc[..., None]
        y = jnp.einsum('bgrts,bsgrp->btgrp', decay * cb[:, :, None], xdt)
        y = y + jnp.exp(cum)[..., None] * jnp.einsum('btgn,bgrnp->btgrp', cc, h)
        to_end = jnp.exp(cum[:, -1:] - cum)[..., None]
        h_new = jnp.exp(cum[:, -1])[..., None, None] * h + jnp.einsum('bsgn,bsgrp->bgrnp', bc, xdt * to_end)
        return h_new, y
    h_fin, ys = lax.scan(step, h0, (chunks(x), chunks(dt), chunks(bm), chunks(cm)))
    return jnp.moveaxis(ys, 0, 1).reshape(x.shape), h_fin


def token_mixers(h, s_gla0, s_ssd0, w_in, gla_dw2, gla_db, gla_norm, w_gla_b, conv_w, conv_b,
                 a_log, dt_bias, d_skip, ssd_norm, w_ssd_b, w_out):
    f32 = jnp.float32
    bsz, t_len, _ = h.shape
    proj = jnp.dot(h, w_in)
    q, k, v, g, dlr, z, xbc, dt_raw, gate_a, gate_b = jnp.split(proj, IN_SPLITS, axis=-1)
    q = q.astype(f32).reshape(bsz, t_len, GLA_HEADS, GLA_DK) * GLA_DK ** -0.5
    k = k.astype(f32).reshape(bsz, t_len, GLA_HEADS, GLA_DK)
    v = v.astype(f32).reshape(bsz, t_len, GLA_HEADS, GLA_DV)
    dlr = dlr.reshape(bsz, t_len, 2, GLA_LOWRANK)
    log_a = jax.nn.log_sigmoid(jnp.einsum('btjr,jrc->btjc', dlr, gla_dw2).astype(f32) + gla_db.astype(f32)) / GLA_TAU
    log_a = log_a.reshape(bsz, t_len, 2, GLA_HEADS, GLA_DK)
    s0 = s_gla0.astype(f32)
    o_f, sg_f = gla_chunk_scan(q, k, v, log_a[:, :, 0], s0[:, 0])
    o_b, sg_b = gla_chunk_scan(q[:, ::-1], k[:, ::-1], v[:, ::-1], log_a[:, ::-1, 1], s0[:, 1])
    o = rms_norm(o_f + o_b[:, ::-1], gla_norm.reshape(GLA_HEADS, GLA_DV)).reshape(bsz, t_len, GLA_V)
    branch_a = jnp.dot(o.astype(h.dtype) * jax.nn.silu(g), w_gla_b)
    xbc = lax.conv_general_dilated(xbc, conv_w[:, None, :].astype(xbc.dtype), window_strides=(1,),
                                   padding=[(CONV_PAD, CONV_PAD)], dimension_numbers=('NWC', 'WIO', 'NWC'),
                                   feature_group_count=CONV_DIM)
    xbc = jax.nn.silu(xbc + conv_b)
    xs, bm, cm = jnp.split(xbc.astype(f32), (D_INNER, D_INNER + SSD_BC), axis=-1)
    xs = xs.reshape(bsz, t_len, SSD_GROUPS, SSD_HPG, SSD_P)
    bm = bm.reshape(bsz, t_len, SSD_GROUPS, SSD_N)
    cm = cm.reshape(bsz, t_len, SSD_GROUPS, SSD_N)
    dt = jax.nn.softplus(dt_raw.astype(f32).reshape(bsz, t_len, 2, SSD_HEADS) + dt_bias.astype(f32))
    dt = dt.reshape(bsz, t_len, 2, SSD_GROUPS, SSD_HPG)
    a = -jnp.exp(a_log.astype(f32)).reshape(2, SSD_GROUPS, SSD_HPG)
    h0 = s_ssd0.astype(f32).reshape(bsz, 2, SSD_GROUPS, SSD_HPG, SSD_N, SSD_P)
    y_f, hs_f = ssd_chunk_scan(xs, dt[:, :, 0], a[0], bm, cm, h0[:, 0])
    y_b, hs_b = ssd_chunk_scan(xs[:, ::-1], dt[:, ::-1, 1], a[1], bm[:, ::-1], cm[:, ::-1], h0[:, 1])
    y = y_f + y_b[:, ::-1] + d_skip.astype(f32).reshape(SSD_GROUPS, SSD_HPG)[..., None] * xs
    y = rms_norm(y.reshape(bsz, t_len, D_INNER) * jax.nn.silu(z.astype(f32)), ssd_norm)
    branch_b = jnp.dot(y.astype(h.dtype), w_ssd_b)
    merged = jax.nn.sigmoid(gate_a) * branch_a + jax.nn.sigmoid(gate_b) * branch_b
    out = jnp.dot(merged, w_out)
    s_gla = jnp.stack([sg_f, sg_b], axis=1)
    s_ssd = jnp.stack([hs_f, hs_b], axis=1).reshape(bsz, 2, SSD_HEADS, SSD_N, SSD_P)
    return out, s_gla, s_ssd


def moe_ffn(h, w_router, router_bias, w1, w3, w2):
    bsz, t_len, d = h.shape
    n_tok = bsz * t_len
    n_asg = n_tok * TOP_K
    t = h.reshape(n_tok, d)
    scores = jax.nn.sigmoid(jnp.dot(t, w_router).astype(jnp.float32))
    biased = scores + router_bias.astype(jnp.float32)
    group_score = lax.top_k(biased.reshape(n_tok, N_EXPERT_GROUPS, EXPERTS_PER_GROUP), TOP_K)[0].sum(-1)
    best_group = jnp.argmax(group_score, axis=-1)
    expert_group = jnp.arange(N_EXPERTS) // EXPERTS_PER_GROUP
    masked = jnp.where(expert_group[None, :] == best_group[:, None], biased, -jnp.inf)
    _, top_idx = lax.top_k(masked, TOP_K)
    top_w = jnp.take_along_axis(scores, top_idx, axis=-1)
    top_w = top_w / jnp.sum(top_w, axis=-1, keepdims=True)
    e_flat = top_idx.reshape(-1)
    tok_flat = jnp.arange(n_asg, dtype=jnp.int32) // TOP_K
    order = jnp.argsort(e_flat)
    e_sorted = e_flat[order]
    counts = jnp.zeros((N_EXPERTS,), jnp.int32).at[e_flat].add(1)
    start = jnp.cumsum(counts) - counts
    padded = (counts + MOE_BLOCK - 1) // MOE_BLOCK * MOE_BLOCK
    padded_end = jnp.cumsum(padded)
    padded_start = padded_end - padded
    dest = padded_start[e_sorted] + jnp.arange(n_asg, dtype=jnp.int32) - start[e_sorted]
    n_blocks = -(-n_asg // MOE_BLOCK) + N_EXPERTS
    n_rows = n_blocks * MOE_BLOCK
    row_tok = jnp.zeros((n_rows,), jnp.int32).at[dest].set(tok_flat[order])
    row_w = jnp.zeros((n_rows,), jnp.float32).at[dest].set(top_w.reshape(-1)[order])
    block_start = jnp.arange(n_blocks, dtype=jnp.int32) * MOE_BLOCK
    block_e = jnp.minimum(jnp.searchsorted(padded_end, block_start, side='right'), N_EXPERTS - 1)
    xb = t[row_tok].reshape(n_blocks, MOE_BLOCK, d)
    def expert_block(args):
        xe, e = args
        hid = jax.nn.silu(jnp.dot(xe, w1[e])) * jnp.dot(xe, w3[e])
        return jnp.dot(hid, w2[e])
    yb = lax.map(expert_block, (xb, block_e)).reshape(n_rows, d)
    y = jnp.zeros((n_tok, d), jnp.float32).at[row_tok].add(yb.astype(jnp.float32) * row_w[:, None])
    return y.reshape(bsz, t_len, d).astype(h.dtype)


def trunk_layer(x, mod, s_gla0, s_ssd0, norm1, norm2, mixer_params, moe_params):
    shift1, scale1, gate1, shift2, scale2, gate2 = [m[:, None, :] for m in jnp.split(mod, 6, axis=-1)]
    h = rms_norm(x, norm1) * (1 + scale1) + shift1
    mix, s_gla, s_ssd = token_mixers(h, s_gla0, s_ssd0, *mixer_params)
    x = x + gate1 * mix
    h = rms_norm(x, norm2) * (1 + scale2) + shift2
    x = x + gate2 * moe_ffn(h, *moe_params)
    return x, s_gla, s_ssd


def setup_inputs(seed: int = 0) -> dict:
    key = jax.random.key(seed)
    ks = jax.random.split(key, 32)
    nrm = lambda k, shape, s: jax.random.normal(k, shape, jnp.float32) * s
    L = DEPTH
    dt0 = jnp.exp(jax.random.uniform(ks[13], (L, 2, SSD_HEADS), jnp.float32,
                                     minval=math.log(DT_MIN), maxval=math.log(DT_MAX)))
    dt_bias = dt0 + jnp.log(-jnp.expm1(-dt0))
    return {
        'x_prompt': nrm(ks[0], (BATCH, SEQ, D_MODEL), 1.0),
        'x_sample': nrm(ks[1], (DEC_BATCH, DEC_SEQ, D_MODEL), 1.0),
        'c': nrm(ks[2], (DEC_BATCH, D_MODEL), 1.0),
        'state_gla': nrm(ks[3], (DEC_BATCH, L, 2, GLA_HEADS, GLA_DK, GLA_DV), 0.5),
        'state_ssd': nrm(ks[4], (DEC_BATCH, L, 2, SSD_HEADS, SSD_N, SSD_P), 0.1),
        'c_ctx': nrm(ks[5], (D_MODEL,), 1.0),
        'w_ada': nrm(ks[6], (L, D_MODEL, 6 * D_MODEL), 0.5 * D_MODEL ** -0.5),
        'b_ada': nrm(ks[7], (L, 6 * D_MODEL), 0.01),
        'norm1': 1.0 + nrm(ks[8], (L, D_MODEL), 0.02),
        'norm2': 1.0 + nrm(ks[9], (L, D_MODEL), 0.02),
        'w_in': nrm(ks[10], (L, D_MODEL, D_IN_PROJ), D_MODEL ** -0.5),
        'gla_dw2': nrm(ks[11], (L, 2, GLA_LOWRANK, GLA_QK), GLA_LOWRANK ** -0.5),
        'gla_db': nrm(ks[12], (L, 2, GLA_QK), 0.1),
        'gla_norm': 1.0 + nrm(ks[14], (L, GLA_V), 0.02),
        'w_gla_b': nrm(ks[15], (L, GLA_V, D_MODEL), GLA_V ** -0.5),
        'conv_w': nrm(ks[16], (L, D_CONV, CONV_DIM), D_CONV ** -0.5),
        'conv_b': nrm(ks[17], (L, CONV_DIM), 0.01),
        'a_log': jnp.log(jax.random.uniform(ks[18], (L, 2, SSD_HEADS), jnp.float32, minval=1.0, maxval=16.0)),
        'dt_bias': dt_bias,
        'd_skip': 1.0 + nrm(ks[19], (L, SSD_HEADS), 0.02),
        'ssd_norm': 1.0 + nrm(ks[20], (L, D_INNER), 0.02),
        'w_ssd_b': nrm(ks[21], (L, D_INNER, D_MODEL), D_INNER ** -0.5),
        'w_out': nrm(ks[22], (L, D_MODEL, D_MODEL), D_MODEL ** -0.5),
        'w_router': nrm(ks[23], (D_MODEL, N_EXPERTS), D_MODEL ** -0.5),
        'router_bias': nrm(ks[24], (N_EXPERTS,), 0.01),
        'w_e1': nrm(ks[25], (L, N_EXPERTS, D_MODEL, D_EXPERT), D_MODEL ** -0.5),
        'w_e3': nrm(ks[26], (L, N_EXPERTS, D_MODEL, D_EXPERT), D_MODEL ** -0.5),
        'w_e2': nrm(ks[27], (L, N_EXPERTS, D_EXPERT, D_MODEL), D_EXPERT ** -0.5),
        'norm_f': 1.0 + nrm(ks[28], (D_MODEL,), 0.02),
    }


def reference(x_prompt, x_sample, c, state_gla, state_ssd, c_ctx, w_ada, b_ada, norm1, norm2, w_in,
              gla_dw2, gla_db, gla_norm, w_gla_b, conv_w, conv_b, a_log, dt_bias, d_skip, ssd_norm,
              w_ssd_b, w_out, w_router, router_bias, w_e1, w_e3, w_e2, norm_f):
    n_ctx = x_prompt.shape[0]
    rows = x_sample.shape[1] // GRID_W
    x_lat = x_sample + sincos_2d(rows, GRID_W, D_MODEL).astype(x_sample.dtype)[None]
    x_ctx = x_prompt
    zero_gla = jnp.zeros((n_ctx, 2, GLA_HEADS, GLA_DK, GLA_DV), jnp.float32)
    zero_ssd = jnp.zeros((n_ctx, 2, SSD_HEADS, SSD_N, SSD_P), jnp.float32)
    act_ctx = jax.nn.silu(c_ctx)[None, :]
    act_lat = jax.nn.silu(c)
    new_gla, new_ssd = [], []
    for l in range(DEPTH):
        mixer_params = (w_in[l], gla_dw2[l], gla_db[l], gla_norm[l], w_gla_b[l], conv_w[l], conv_b[l],
                        a_log[l], dt_bias[l], d_skip[l], ssd_norm[l], w_ssd_b[l], w_out[l])
        moe_params = (w_router, router_bias, w_e1[l], w_e3[l], w_e2[l])
        mod_ctx = jnp.dot(act_ctx, w_ada[l]) + b_ada[l]
        mod_lat = jnp.dot(act_lat, w_ada[l]) + b_ada[l]
        x_ctx, s_gla, s_ssd = trunk_layer(x_ctx, mod_ctx, zero_gla, zero_ssd, norm1[l], norm2[l],
                                          mixer_params, moe_params)
        new_gla.append(s_gla)
        new_ssd.append(s_ssd)
        x_lat, _, _ = trunk_layer(x_lat, mod_lat, state_gla[:, l], state_ssd[:, l], norm1[l], norm2[l],
                                  mixer_params, moe_params)
    y_prompt = rms_norm(x_ctx, norm_f)
    y_sample = rms_norm(x_lat, norm_f)
    new_state_gla = jnp.stack(new_gla, axis=1)
    new_state_ssd = jnp.stack(new_ssd, axis=1)
    return (y_prompt, y_sample, new_state_gla, new_state_ssd)
```

```python
import functools
import math
from typing import NamedTuple

import jax
import jax.numpy as jnp
from jax import lax
from jax.experimental import pallas as pl
from jax.experimental.pallas import tpu as pltpu

F32 = jnp.float32
BF16 = jnp.bfloat16

D = 2048
GRID_W = 64
POS_BASE = 10000.0
EPS = 1e-6
GLA_H = 4
GLA_DK = D // 2 // GLA_H
GLA_DV = D // GLA_H
GLA_QK = GLA_H * GLA_DK
GLA_V = GLA_H * GLA_DV
GLA_LR = 16
GLA_TAU = 16.0
GLA_C = 64
D_INNER = 2 * D
SSD_P = 64
SSD_HEADS = D_INNER // SSD_P
SSD_N = 128
SSD_G = 8
SSD_HPG = SSD_HEADS // SSD_G
SSD_BC = SSD_G * SSD_N
CONV_DIM = D_INNER + 2 * SSD_BC
D_CONV = 5
SSD_C = 128
N_EXP = 16
N_GRP = 4
EPG = N_EXP // N_GRP
TOP_K = 2
D_EXP = D // 2
MOE_BLOCK = 256

LANE = 128
SUBLANE = 8
RB = 256
VMEM_LIMIT = 56 * 1024 * 1024

C_Q = 0
C_K = C_Q + GLA_QK
C_V = C_K + GLA_QK
C_G = C_V + GLA_V
C_GA = C_G + GLA_V
C_Z = C_GA + D
C_XBC = C_Z + D_INNER
C_B = C_XBC + D_INNER
C_CM = C_B + SSD_BC
C_GB = C_XBC + CONV_DIM
C_DT = C_GB + D
C_DLR = C_DT + 2 * SSD_HEADS
NP = C_DLR + 3 * LANE


class Layout(NamedTuple):
    n_ctx: int
    t_ctx: int
    n_lat: int
    t_lat: int

    @property
    def m_ctx(self):
        return self.n_ctx * self.t_ctx

    @property
    def m(self):
        return self.m_ctx + self.n_lat * self.t_lat

    @property
    def nb(self):
        return self.m // RB

    @property
    def nb_ctx(self):
        return self.m_ctx // RB

    @property
    def bps_ctx(self):
        return self.t_ctx // RB

    @property
    def bps_lat(self):
        return self.t_lat // RB


def _seq_info(lay, blk):
    is_ctx = blk < lay.nb_ctx
    lb = jnp.maximum(blk - lay.nb_ctx, 0)
    cb = jnp.minimum(blk, lay.nb_ctx - 1)
    pos = jnp.where(is_ctx, cb % lay.bps_ctx, lb % lay.bps_lat)
    bps = jnp.where(is_ctx, lay.bps_ctx, lay.bps_lat)
    return is_ctx, cb // lay.bps_ctx, jnp.minimum(lb // lay.bps_lat, lay.n_lat - 1), pos, bps


def _mod_row(lay, row_start):
    return jnp.where(row_start < lay.m_ctx, 0, 1 + jnp.maximum(row_start - lay.m_ctx, 0) // lay.t_lat)


def _cparams(sem):
    return pltpu.CompilerParams(dimension_semantics=sem, vmem_limit_bytes=VMEM_LIMIT)


def _silu(x):
    return x * jax.nn.sigmoid(x)


def _softplus(x):
    return jnp.maximum(x, 0.0) + jnp.log1p(jnp.exp(-jnp.abs(x)))


def _cumsum_rows(x, n, reverse):
    rows = lax.broadcasted_iota(jnp.int32, x.shape, 0)
    s = 1
    while s < n:
        if reverse:
            x = x + jnp.where(rows < n - s, pltpu.roll(x, n - s, 0), 0.0)
        else:
            x = x + jnp.where(rows >= s, pltpu.roll(x, s, 0), 0.0)
        s *= 2
    return x


NT = (((1,), (1,)), ((), ()))
TN = (((0,), (0,)), ((), ()))


def _embed_kernel(xp_ref, xs_ref, pos_ref, o_ref, *, lay):
    rb = pl.program_id(0)

    @pl.when(rb < lay.nb_ctx)
    def _():
        o_ref[...] = xp_ref[...]

    @pl.when(rb >= lay.nb_ctx)
    def _():
        o_ref[...] = xs_ref[...] + pos_ref[...]


def _embed(lay, xp, xs, pos):
    spec = lambda f: pl.BlockSpec((RB, D), f)
    return pl.pallas_call(
        functools.partial(_embed_kernel, lay=lay),
        out_shape=jax.ShapeDtypeStruct((lay.m, D), F32),
        grid=(lay.nb,),
        in_specs=[spec(lambda i: (jnp.minimum(i, lay.nb_ctx - 1), 0)),
                  spec(lambda i: (jnp.maximum(i - lay.nb_ctx, 0), 0)),
                  spec(lambda i: (jnp.maximum(i - lay.nb_ctx, 0) % lay.bps_lat, 0))],
        out_specs=spec(lambda i: (i, 0)),
        compiler_params=_cparams(("arbitrary",)),
        name="embed",
    )(xp, xs, pos)


def _fused_mm_kernel(*refs, n_rows, n_vecs, n_erows, n_evecs, prologue, epilogue, tm, rc):
    rows = refs[:n_rows]
    vecs = refs[n_rows:n_rows + n_vecs]
    w_ref = refs[n_rows + n_vecs]
    p = n_rows + n_vecs + 1
    erows = refs[p:p + n_erows]
    evecs = refs[p + n_erows:p + n_erows + n_evecs]
    o_ref = refs[p + n_erows + n_evecs]

    if prologue is None:
        lhs = rows[0][...]
    else:
        h_ref = refs[p + n_erows + n_evecs + 1]

        @pl.when(pl.program_id(1) == 0)
        def _():
            vs = [v[...] for v in vecs]

            def chunk(c, carry):
                sl = pl.ds(pl.multiple_of(c * rc, rc), rc)
                h_ref[sl, :] = prologue([r[sl, :] for r in rows], vs).astype(BF16)
                return carry

            lax.fori_loop(0, tm // rc, chunk, 0)

        lhs = h_ref[...]
    acc = jnp.dot(lhs, w_ref[...], preferred_element_type=F32)
    if epilogue is not None:
        acc = epilogue(acc, [e[...] for e in erows], [v[...] for v in evecs])
    o_ref[...] = acc.astype(o_ref.dtype)


def _fused_mm(name, m, k, n, *, tm, tn, rows, vecs, w, w_col0=0, erows=(), evecs=(), prologue, epilogue,
              out_dtype, rc=32):
    in_specs, args = [], []
    for arr, width, cb in rows:
        in_specs.append(pl.BlockSpec((tm, width), lambda i, j, cb=cb: (i, cb)))
        args.append(arr)
    for arr, width, fn in vecs:
        in_specs.append(pl.BlockSpec((None, 1, width), lambda i, j, fn=fn: (fn(i), 0, 0)))
        args.append(arr)
    in_specs.append(pl.BlockSpec((k, tn), lambda i, j: (0, w_col0 + j)))
    args.append(w)
    for arr, cb0 in erows:
        in_specs.append(pl.BlockSpec((tm, tn), lambda i, j, cb0=cb0: (i, cb0 + j)))
        args.append(arr)
    for arr, fn in evecs:
        in_specs.append(pl.BlockSpec((None, 1, tn), lambda i, j, fn=fn: (fn(i), 0, j)))
        args.append(arr)
    scratch = [] if prologue is None else [pltpu.VMEM((tm, k), BF16)]
    kern = functools.partial(_fused_mm_kernel, n_rows=len(rows), n_vecs=len(vecs), n_erows=len(erows),
                             n_evecs=len(evecs), prologue=prologue, epilogue=epilogue, tm=tm, rc=rc)
    return pl.pallas_call(
        kern,
        out_shape=jax.ShapeDtypeStruct((m, n), out_dtype),
        grid=(m // tm, n // tn),
        in_specs=in_specs,
        out_specs=pl.BlockSpec((tm, tn), lambda i, j: (i, j)),
        scratch_shapes=scratch,
        compiler_params=_cparams(("arbitrary", "arbitrary")),
        name=name,
    )(*args)


def _rms(x, gain):
    return x * lax.rsqrt(jnp.mean(x * x, axis=-1, keepdims=True) + EPS) * gain


def _pro_norm_mod(rows, vecs):
    (x,), (gain, scale, shift) = rows, vecs
    return _rms(x, gain) * (1.0 + scale) + shift


def _pro_silu(rows, vecs):
    return _silu(rows[0])


def _pro_gla_out(rows, vecs):
    (of, ob, g), (gain,) = rows, vecs
    o = of + ob
    parts = [_rms(o[:, h * GLA_DV:(h + 1) * GLA_DV], gain[:, h * GLA_DV:(h + 1) * GLA_DV]) for h in range(GLA_H)]
    return jnp.concatenate(parts, axis=1) * _silu(g)


def _pro_ssd_out(rows, vecs):
    (yf, yb, z), (gain,) = rows, vecs
    return _rms((yf + yb) * _silu(z), gain)


def _epi_bias(acc, erows, evecs):
    return acc + evecs[0]


def _epi_gate_a(acc, erows, evecs):
    return jax.nn.sigmoid(erows[0]) * acc


def _epi_gate_b(acc, erows, evecs):
    return erows[0] + jax.nn.sigmoid(erows[1]) * acc


def _epi_resid(acc, erows, evecs):
    return erows[0] + evecs[0] * acc


def _gla_kernel(q_ref, k_ref, v_ref, dlr_ref, dw_ref, db_ref, s0_ref, o_ref, sf_ref, s_ref, *, lay, reverse):
    i = pl.program_id(1)
    blk = (lay.nb - 1 - i) if reverse else i
    is_ctx, _, _, pos, bps = _seq_info(lay, blk)
    first = (pos == bps - 1) if reverse else (pos == 0)
    last = (pos == 0) if reverse else (pos == bps - 1)

    @pl.when(first & is_ctx)
    def _():
        s_ref[...] = jnp.zeros_like(s_ref)

    @pl.when(first & jnp.logical_not(is_ctx))
    def _():
        s_ref[...] = s0_ref[...]

    x = jnp.dot(dlr_ref[...].astype(BF16), dw_ref[...], preferred_element_type=F32) + db_ref[...]
    log_a = (jnp.minimum(x, 0.0) - jnp.log1p(jnp.exp(-jnp.abs(x)))) * (1.0 / GLA_TAU)

    r = lax.broadcasted_iota(jnp.int32, (GLA_C, GLA_C), 0)
    c = lax.broadcasted_iota(jnp.int32, (GLA_C, GLA_C), 1)
    mask = (c >= r) if reverse else (c <= r)
    n_chunks = RB // GLA_C
    order = range(n_chunks - 1, -1, -1) if reverse else range(n_chunks)
    for ci in order:
        sl = slice(ci * GLA_C, (ci + 1) * GLA_C)
        b = _cumsum_rows(log_a[sl], GLA_C, reverse)
        q = q_ref[sl, :] * (GLA_DK ** -0.5)
        k = k_ref[sl, :]
        v = v_ref[sl, :].astype(BF16)
        qe = (q * jnp.exp(b)).astype(BF16)
        ke = (k * jnp.exp(-b)).astype(BF16)
        att = jnp.where(mask, lax.dot_general(qe, ke, NT, preferred_element_type=F32), 0.0)
        s = s_ref[...]
        o = jnp.dot(att.astype(BF16), v, preferred_element_type=F32)
        o = o + jnp.dot(qe, s.astype(BF16), preferred_element_type=F32)
        o_ref[sl, :] = o
        b_end = b[0:1, :] if reverse else b[GLA_C - 1:GLA_C, :]
        kd = (k * jnp.exp(b_end - b)).astype(BF16)
        e_col = jnp.exp(jnp.transpose(jnp.broadcast_to(b_end, (LANE, GLA_DK))))
        upd = lax.dot_general(kd, v, TN, preferred_element_type=F32)
        s_ref[...] = s * jnp.concatenate([e_col] * (GLA_DV // LANE), axis=1) + upd

    @pl.when(last & is_ctx)
    def _():
        sf_ref[...] = s_ref[...]


def _gla_scan(lay, proj, dw2p, db, s0, *, reverse, name):
    d = 1 if reverse else 0
    blk = (lambda i: lay.nb - 1 - i) if reverse else (lambda i: i)
    cseq = lambda i: _seq_info(lay, blk(i))[1]
    lseq = lambda i: _seq_info(lay, blk(i))[2]
    return pl.pallas_call(
        functools.partial(_gla_kernel, lay=lay, reverse=reverse),
        out_shape=(jax.ShapeDtypeStruct((lay.m, GLA_V), F32),
                   jax.ShapeDtypeStruct((lay.n_ctx, GLA_H, GLA_DK, GLA_DV), F32)),
        grid=(GLA_H, lay.nb),
        in_specs=[
            pl.BlockSpec((RB, GLA_DK), lambda h, i: (blk(i), C_Q // GLA_DK + h)),
            pl.BlockSpec((RB, GLA_DK), lambda h, i: (blk(i), C_K // GLA_DK + h)),
            pl.BlockSpec((RB, GLA_DV), lambda h, i: (blk(i), C_V // GLA_DV + h)),
            pl.BlockSpec((RB, LANE), lambda h, i: (blk(i), C_DLR // LANE + d)),
            pl.BlockSpec((None, LANE, GLA_DK), lambda h, i: (d, 0, h)),
            pl.BlockSpec((None, 1, GLA_DK), lambda h, i: (d, 0, h)),
            pl.BlockSpec((None, None, None, GLA_DK, GLA_DV), lambda h, i: (lseq(i), d, h, 0, 0)),
        ],
        out_specs=(pl.BlockSpec((RB, GLA_DV), lambda h, i: (blk(i), h)),
                   pl.BlockSpec((None, None, GLA_DK, GLA_DV), lambda h, i: (cseq(i), h, 0, 0))),
        scratch_shapes=[pltpu.VMEM((GLA_DK, GLA_DV), F32)],
        compiler_params=_cparams(("arbitrary", "arbitrary")),
        name=name,
    )(proj, proj, proj, proj, dw2p, db, s0)


CONV_TC = 512


def _conv_kernel(prev_ref, cur_ref, next_ref, w_ref, b_ref, o_ref, *, lay):
    rb = pl.program_id(0)
    _, _, _, pos, bps = _seq_info(lay, rb)
    prev = jnp.where(pos > 0, prev_ref[...], 0.0)
    nxt = jnp.where(pos < bps - 1, next_ref[...], 0.0)
    ext = jnp.concatenate([prev, cur_ref[...], nxt], axis=0)
    n = RB + 2 * SUBLANE
    w = w_ref[...]
    acc = jnp.broadcast_to(b_ref[...], (RB, CONV_TC))
    for j in range(D_CONV):
        off = j - D_CONV // 2
        sh = ext if off == 0 else pltpu.roll(ext, (-off) % n, 0)
        acc = acc + w[j:j + 1, :] * sh[SUBLANE:SUBLANE + RB, :]
    o_ref[...] = _silu(acc)


def _conv(lay, proj, cw, cb):
    c0 = C_XBC // CONV_TC
    rpb = RB // SUBLANE
    return pl.pallas_call(
        functools.partial(_conv_kernel, lay=lay),
        out_shape=jax.ShapeDtypeStruct((lay.m, CONV_DIM), F32),
        grid=(lay.nb, CONV_DIM // CONV_TC),
        in_specs=[
            pl.BlockSpec((SUBLANE, CONV_TC), lambda i, j: (jnp.maximum(i * rpb - 1, 0), c0 + j)),
            pl.BlockSpec((RB, CONV_TC), lambda i, j: (i, c0 + j)),
            pl.BlockSpec((SUBLANE, CONV_TC), lambda i, j: (jnp.minimum((i + 1) * rpb, lay.m // SUBLANE - 1), c0 + j)),
            pl.BlockSpec((D_CONV, CONV_TC), lambda i, j: (0, j)),
            pl.BlockSpec((1, CONV_TC), lambda i, j: (0, j)),
        ],
        out_specs=pl.BlockSpec((RB, CONV_TC), lambda i, j: (i, j)),
        compiler_params=_cparams(("arbitrary", "arbitrary")),
        name="conv",
    )(proj, proj, proj, cw, cb)


def _dtprep_kernel(raw_ref, bias_ref, alog_ref, dt_ref, cum_ref):
    dt = _softplus(raw_ref[...] + bias_ref[...])
    dta = dt * (-jnp.exp(alog_ref[...]))
    lane = lax.broadcasted_iota(jnp.int32, (SSD_C, LANE), 1)
    cums = []
    for ci in range(RB // SSD_C):
        t = dta[ci * SSD_C:(ci + 1) * SSD_C]
        cums.append(jnp.where(lane < SSD_HEADS, _cumsum_rows(t, SSD_C, False), _cumsum_rows(t, SSD_C, True)))
    cum = jnp.concatenate(cums, axis=0)
    for d in range(2):
        for g in range(SSD_G):
            base = d * SSD_HEADS + g * SSD_HPG
            sh = (LANE - base) % LANE
            dt_ref[d, g] = dt if sh == 0 else pltpu.roll(dt, sh, 1)
            cum_ref[d, g] = cum if sh == 0 else pltpu.roll(cum, sh, 1)


def _dtprep(lay, proj, dt_bias, a_log):
    out = jax.ShapeDtypeStruct((2, SSD_G, lay.m, LANE), F32)
    ospec = pl.BlockSpec((2, SSD_G, RB, LANE), lambda i: (0, 0, i, 0))
    vspec = pl.BlockSpec((1, LANE), lambda i: (0, 0))
    return pl.pallas_call(
        _dtprep_kernel,
        out_shape=(out, out),
        grid=(lay.nb,),
        in_specs=[pl.BlockSpec((RB, LANE), lambda i: (i, C_DT // LANE)), vspec, vspec],
        out_specs=(ospec, ospec),
        compiler_params=_cparams(("arbitrary",)),
        name="dtprep",
    )(proj, dt_bias, a_log)


def _ssd_kernel(x_ref, b_ref, c_ref, dt_ref, cum_ref, h0_ref, dsk_ref, y_ref, hf_ref, h_ref, *, lay, reverse):
    i = pl.program_id(1)
    blk = (lay.nb - 1 - i) if reverse else i
    is_ctx, _, _, pos, bps = _seq_info(lay, blk)
    first = (pos == bps - 1) if reverse else (pos == 0)
    last = (pos == 0) if reverse else (pos == bps - 1)

    @pl.when(first & is_ctx)
    def _():
        h_ref[...] = jnp.zeros_like(h_ref)

    @pl.when(first & jnp.logical_not(is_ctx))
    def _():
        h_ref[...] = h0_ref[...]

    r = lax.broadcasted_iota(jnp.int32, (SSD_C, SSD_C), 0)
    c = lax.broadcasted_iota(jnp.int32, (SSD_C, SSD_C), 1)
    mask = (c >= r) if reverse else (c <= r)
    lo = c < SSD_P
    n_chunks = RB // SSD_C
    order = range(n_chunks - 1, -1, -1) if reverse else range(n_chunks)
    n_pairs = SSD_HPG // 2
    for ci in order:
        sl = slice(ci * SSD_C, (ci + 1) * SSD_C)
        x = x_ref[sl, :]
        bm = b_ref[sl, :].astype(BF16)
        cm = c_ref[sl, :].astype(BF16)
        dt = dt_ref[sl, :]
        cum = cum_ref[sl, :]
        cum_t = jnp.transpose(cum)
        cb = lax.dot_general(cm, bm, NT, preferred_element_type=F32)
        h = h_ref[...]
        y_inter = jnp.dot(cm, h.astype(BF16), preferred_element_type=F32)
        e_idx = 0 if reverse else SSD_C - 1
        ys, xdes, decs = [], [], []
        for j in range(n_pairs):
            cols, ls = [], []
            for rr in (2 * j, 2 * j + 1):
                col = jnp.broadcast_to(cum[:, rr:rr + 1], (SSD_C, SSD_C))
                seg = col - cum_t[rr:rr + 1, :]
                ls.append((jnp.exp(jnp.where(mask, seg, -1e30)) * cb).astype(BF16))
                cols.append(col)
            cp = jnp.where(lo, cols[0], cols[1])
            dtp = jnp.where(lo, jnp.broadcast_to(dt[:, 2 * j:2 * j + 1], (SSD_C, SSD_C)),
                            jnp.broadcast_to(dt[:, 2 * j + 1:2 * j + 2], (SSD_C, SSD_C)))
            xp = x[:, j * LANE:(j + 1) * LANE]
            xdt = xp * dtp
            rhs = jnp.concatenate([jnp.where(lo, xdt, 0.0), jnp.where(lo, 0.0, xdt)], axis=0).astype(BF16)
            lhs = jnp.concatenate(ls, axis=1)
            y = jnp.dot(lhs, rhs, preferred_element_type=F32)
            y = y + jnp.exp(cp) * y_inter[:, j * LANE:(j + 1) * LANE]
            if not reverse:
                y = y + dsk_ref[:, j * LANE:(j + 1) * LANE] * xp
            ys.append(y)
            edge = cp[e_idx:e_idx + 1, :]
            xdes.append((xdt * jnp.exp(edge - cp)).astype(BF16))
            decs.append(jnp.exp(edge))
        y_ref[sl, :] = jnp.concatenate(ys, axis=1)
        upd = lax.dot_general(bm, jnp.concatenate(xdes, axis=1), TN, preferred_element_type=F32)
        h_ref[...] = h * jnp.concatenate(decs, axis=1) + upd

    @pl.when(last & is_ctx)
    def _():
        hf_ref[...] = h_ref[...]


def _ssd_scan(lay, xbc, dtg, cumg, h0, dsk, *, reverse, name):
    d = 1 if reverse else 0
    blk = (lambda i: lay.nb - 1 - i) if reverse else (lambda i: i)
    cseq = lambda i: _seq_info(lay, blk(i))[1]
    lseq = lambda i: _seq_info(lay, blk(i))[2]
    gw = SSD_HPG * SSD_P
    return pl.pallas_call(
        functools.partial(_ssd_kernel, lay=lay, reverse=reverse),
        out_shape=(jax.ShapeDtypeStruct((lay.m, D_INNER), F32),
                   jax.ShapeDtypeStruct((lay.n_ctx, SSD_G, SSD_N, gw), F32)),
        grid=(SSD_G, lay.nb),
        in_specs=[
            pl.BlockSpec((RB, gw), lambda g, i: (blk(i), g)),
            pl.BlockSpec((RB, SSD_N), lambda g, i: (blk(i), D_INNER // SSD_N + g)),
            pl.BlockSpec((RB, SSD_N), lambda g, i: (blk(i), (D_INNER + SSD_BC) // SSD_N + g)),
            pl.BlockSpec((None, None, RB, LANE), lambda g, i: (d, g, blk(i), 0)),
            pl.BlockSpec((None, None, RB, LANE), lambda g, i: (d, g, blk(i), 0)),
            pl.BlockSpec((None, None, None, SSD_N, gw), lambda g, i: (lseq(i), d, g, 0, 0)),
            pl.BlockSpec((1, gw), lambda g, i: (0, g)),
        ],
        out_specs=(pl.BlockSpec((RB, gw), lambda g, i: (blk(i), g)),
                   pl.BlockSpec((None, None, SSD_N, gw), lambda g, i: (cseq(i), g, 0, 0))),
        scratch_shapes=[pltpu.VMEM((SSD_N, gw), F32)],
        compiler_params=_cparams(("arbitrary", "arbitrary")),
        name=name,
    )(xbc, xbc, xbc, dtg, cumg, h0, dsk)


def _split3(x):
    hi = x.astype(BF16)
    r1 = x - hi.astype(F32)
    mid = r1.astype(BF16)
    lo = (r1 - mid.astype(F32)).astype(BF16)
    return hi, mid, lo


def _norm_router_kernel(x_ref, gain_ref, scale_ref, shift_ref, wr_ref, h_ref, lg_ref):
    h = _rms(x_ref[...], gain_ref[...]) * (1.0 + scale_ref[...]) + shift_ref[...]
    h_ref[...] = h.astype(BF16)
    hs = _split3(h)
    ws = _split3(wr_ref[...])
    acc = jnp.zeros(lg_ref.shape, F32)
    for a, b in ((2, 0), (0, 2), (1, 1), (1, 0), (0, 1), (0, 0)):
        acc = acc + jnp.dot(hs[a], ws[b], preferred_element_type=F32)
    lg_ref[...] = acc


def _norm_router(lay, x, gain, mod3, wr):
    mrow = lambda i: _mod_row(lay, i * RB)
    vspec = lambda fn: pl.BlockSpec((None, 1, D), lambda i: (fn(i), 0, 0))
    return pl.pallas_call(
        _norm_router_kernel,
        out_shape=(jax.ShapeDtypeStruct((lay.m, D), BF16), jax.ShapeDtypeStruct((lay.m, LANE), F32)),
        grid=(lay.nb,),
        in_specs=[pl.BlockSpec((RB, D), lambda i: (i, 0)), vspec(lambda i: 0),
                  vspec(lambda i: mrow(i) * 6 + 4), vspec(lambda i: mrow(i) * 6 + 3),
                  pl.BlockSpec((D, LANE), lambda i: (0, 0))],
        out_specs=(pl.BlockSpec((RB, D), lambda i: (i, 0)), pl.BlockSpec((RB, LANE), lambda i: (i, 0))),
        compiler_params=_cparams(("arbitrary",)),
        name="norm_router",
    )(x, gain, mod3, mod3, wr)


def _expert_kernel(be_ref, nu_ref, x_ref, w1_ref, w3_ref, w2_ref, rw_ref, o_ref):
    b = pl.program_id(0)

    @pl.when(b < nu_ref[0])
    def _():
        x = x_ref[...]
        h1 = jnp.dot(x, w1_ref[...], preferred_element_type=F32)
        h3 = jnp.dot(x, w3_ref[...], preferred_element_type=F32)
        hid = (_silu(h1) * h3).astype(BF16)
        o_ref[...] = jnp.dot(hid, w2_ref[...], preferred_element_type=F32) * rw_ref[...]

    @pl.when(b >= nu_ref[0])
    def _():
        o_ref[...] = jnp.zeros_like(o_ref)


def _experts(xb, w1, w3, w2, row_w, block_e, n_used):
    n_rows = xb.shape[0]
    n_blocks = n_rows // MOE_BLOCK
    wspec = lambda a, b: pl.BlockSpec((None, a, b), lambda i, be, nu: (be[i], 0, 0))
    return pl.pallas_call(
        _expert_kernel,
        out_shape=jax.ShapeDtypeStruct((n_rows, D), F32),
        grid_spec=pltpu.PrefetchScalarGridSpec(
            num_scalar_prefetch=2,
            grid=(n_blocks,),
            in_specs=[pl.BlockSpec((MOE_BLOCK, D), lambda i, be, nu: (i, 0)),
                      wspec(D, D_EXP), wspec(D, D_EXP), wspec(D_EXP, D),
                      pl.BlockSpec((MOE_BLOCK, 1), lambda i, be, nu: (i, 0))],
            out_specs=pl.BlockSpec((MOE_BLOCK, D), lambda i, be, nu: (i, 0)),
        ),
        compiler_params=_cparams(("arbitrary",)),
        name="experts",
    )(block_e, n_used, xb, w1, w3, w2, row_w)


def _combine_kernel(x_ref, y0_ref, y1_ref, gate_ref, o_ref):
    o_ref[...] = x_ref[...] + gate_ref[...] * (y0_ref[...] + y1_ref[...])


def _combine(lay, x, y0, y1, mod3):
    mrow = lambda i: _mod_row(lay, i * RB)
    spec = pl.BlockSpec((RB, D), lambda i: (i, 0))
    return pl.pallas_call(
        _combine_kernel,
        out_shape=jax.ShapeDtypeStruct((lay.m, D), F32),
        grid=(lay.nb,),
        in_specs=[spec, spec, spec, pl.BlockSpec((None, 1, D), lambda i: (mrow(i) * 6 + 5, 0, 0))],
        out_specs=spec,
        compiler_params=_cparams(("arbitrary",)),
        name="combine",
    )(x, y0, y1, mod3)


def _final_kernel(x_ref, y0_ref, y1_ref, gate_ref, nf_ref, op_ref, os_ref, *, lay):
    rb = pl.program_id(0)
    x = x_ref[...] + gate_ref[...] * (y0_ref[...] + y1_ref[...])
    y = _rms(x, nf_ref[...])

    @pl.when(rb < lay.nb_ctx)
    def _():
        op_ref[...] = y

    @pl.when(rb >= lay.nb_ctx)
    def _():
        os_ref[...] = y


def _final(lay, x, y0, y1, mod3, norm_f):
    mrow = lambda i: _mod_row(lay, i * RB)
    spec = pl.BlockSpec((RB, D), lambda i: (i, 0))
    return pl.pallas_call(
        functools.partial(_final_kernel, lay=lay),
        out_shape=(jax.ShapeDtypeStruct((lay.m_ctx, D), F32), jax.ShapeDtypeStruct((lay.m - lay.m_ctx, D), F32)),
        grid=(lay.nb,),
        in_specs=[spec, spec, spec, pl.BlockSpec((None, 1, D), lambda i: (mrow(i) * 6 + 5, 0, 0)),
                  pl.BlockSpec((1, D), lambda i: (0, 0))],
        out_specs=(pl.BlockSpec((RB, D), lambda i: (jnp.minimum(i, lay.nb_ctx - 1), 0)),
                   pl.BlockSpec((RB, D), lambda i: (jnp.maximum(i - lay.nb_ctx, 0), 0))),
        compiler_params=_cparams(("arbitrary",)),
        name="final",
    )(x, y0, y1, mod3, norm_f)


def _route(logits, router_bias, n_tok):
    n_asg = n_tok * TOP_K
    scores = jax.nn.sigmoid(logits)
    biased = scores + router_bias.astype(F32)
    group_score = lax.top_k(biased.reshape(n_tok, N_GRP, EPG), TOP_K)[0].sum(-1)
    best_group = jnp.argmax(group_score, axis=-1)
    expert_group = jnp.arange(N_EXP) // EPG
    masked = jnp.where(expert_group[None, :] == best_group[:, None], biased, -jnp.inf)
    _, top_idx = lax.top_k(masked, TOP_K)
    top_w = jnp.take_along_axis(scores, top_idx, axis=-1)
    top_w = top_w / jnp.sum(top_w, axis=-1, keepdims=True)
    e_flat = top_idx.reshape(-1)
    tok_flat = jnp.arange(n_asg, dtype=jnp.int32) // TOP_K
    order = jnp.argsort(e_flat)
    e_sorted = e_flat[order]
    counts = jnp.zeros((N_EXP,), jnp.int32).at[e_flat].add(1)
    start = jnp.cumsum(counts) - counts
    padded = (counts + MOE_BLOCK - 1) // MOE_BLOCK * MOE_BLOCK
    padded_end = jnp.cumsum(padded)
    padded_start = padded_end - padded
    dest = padded_start[e_sorted] + jnp.arange(n_asg, dtype=jnp.int32) - start[e_sorted]
    n_blocks = -(-n_asg // MOE_BLOCK) + N_EXP
    n_rows = n_blocks * MOE_BLOCK
    row_tok = jnp.zeros((n_rows,), jnp.int32).at[dest].set(tok_flat[order])
    row_w = jnp.zeros((n_rows,), F32).at[dest].set(top_w.reshape(-1)[order])
    block_start = jnp.arange(n_blocks, dtype=jnp.int32) * MOE_BLOCK
    block_e = jnp.minimum(jnp.searchsorted(padded_end, block_start, side='right'), N_EXP - 1).astype(jnp.int32)
    n_used = (padded_end[-1] // MOE_BLOCK).astype(jnp.int32).reshape(1)
    pos = jnp.zeros((n_asg,), jnp.int32).at[order].set(dest).reshape(n_tok, TOP_K)
    return row_tok, row_w, block_e, n_used, pos


def _sincos_2d(rows, cols, dim):
    quarter = dim // 4
    omega = 1.0 / (POS_BASE ** (jnp.arange(quarter, dtype=F32) / quarter))

    def axis_embed(n):
        ang = jnp.arange(n, dtype=F32)[:, None] * omega[None, :]
        return jnp.concatenate([jnp.sin(ang), jnp.cos(ang)], axis=-1)

    er = jnp.broadcast_to(axis_embed(rows)[:, None, :], (rows, cols, dim // 2))
    ec = jnp.broadcast_to(axis_embed(cols)[None, :, :], (rows, cols, dim // 2))
    return jnp.concatenate([er, ec], axis=-1).reshape(rows * cols, dim)


def _rearranged_w_in(w):
    sizes = (GLA_QK, GLA_QK, GLA_V, GLA_V, 2 * GLA_LR, D_INNER, CONV_DIM, 2 * SSD_HEADS, D, D)
    offs = [0]
    for s in sizes:
        offs.append(offs[-1] + s)
    q, k, v, g, dlr, z, xbc, dt, ga, gb = [w[:, offs[i]:offs[i + 1]] for i in range(len(sizes))]
    zpad = jnp.zeros((w.shape[0], LANE - GLA_LR), w.dtype)
    return jnp.concatenate([q, k, v, g, ga, z, xbc, gb, dt, dlr[:, :GLA_LR], zpad, dlr[:, GLA_LR:], zpad,
                            jnp.zeros((w.shape[0], LANE), w.dtype)], axis=1).astype(BF16)


def _pick_tile(lay, want):
    t = want
    while lay.m_ctx % t or lay.t_lat % t:
        t //= 2
    return t


def kernel(x_prompt, x_sample, c, state_gla, state_ssd, c_ctx, w_ada, b_ada, norm1, norm2, w_in, gla_dw2, gla_db, gla_norm, w_gla_b, conv_w, conv_b, a_log, dt_bias, d_skip, ssd_norm, w_ssd_b, w_out, w_router, router_bias, w_e1, w_e3, w_e2, norm_f):
    n_ctx, t_ctx, _ = x_prompt.shape
    n_lat, t_lat, _ = x_sample.shape
    lay = Layout(n_ctx, t_ctx, n_lat, t_lat)
    depth = w_in.shape[0]
    m = lay.m
    assert t_ctx % RB == 0 and t_lat % RB == 0 and t_lat % GRID_W == 0

    pos = _sincos_2d(t_lat // GRID_W, GRID_W, D)
    x = _embed(lay, x_prompt.reshape(lay.m_ctx, D), x_sample.reshape(n_lat * t_lat, D), pos)

    n_mod = 16
    cond = jnp.concatenate([c_ctx[None, :], c, jnp.zeros((n_mod - 1 - n_lat, D), F32)], axis=0)
    wr = jnp.pad(w_router, ((0, 0), (0, LANE - N_EXP)))

    tm_in = _pick_tile(lay, 1024)
    tm_a = _pick_tile(lay, 512)
    mrow = lambda tm: (lambda i: _mod_row(lay, i * tm))
    new_gla, new_ssd = [], []
    y_prompt = y_sample = None
    for l in range(depth):
        mod = _fused_mm("ada", n_mod, D, 6 * D, tm=n_mod, tn=1024, rows=[(cond, D, 0)], vecs=[],
                        w=w_ada[l].astype(BF16), evecs=[(b_ada[l].reshape(1, 1, 6 * D), lambda i: 0)],
                        prologue=_pro_silu, epilogue=_epi_bias, out_dtype=F32, rc=n_mod)
        mod3 = mod.reshape(n_mod * 6, 1, D)
        n1 = norm1[l].reshape(1, 1, D)
        mr = mrow(tm_in)
        proj = _fused_mm("in_proj", m, D, NP, tm=tm_in, tn=512, rows=[(x, D, 0)],
                         vecs=[(n1, D, lambda i: 0), (mod3, D, lambda i: mr(i) * 6 + 1),
                               (mod3, D, lambda i: mr(i) * 6 + 0)],
                         w=_rearranged_w_in(w_in[l]), prologue=_pro_norm_mod, epilogue=None, out_dtype=F32)

        dw2p = jnp.pad(gla_dw2[l], ((0, 0), (0, LANE - GLA_LR), (0, 0))).astype(BF16)
        db = gla_db[l].reshape(2, 1, GLA_QK)
        sg0 = state_gla[:, l]
        o_f, sg_f = _gla_scan(lay, proj, dw2p, db, sg0, reverse=False, name="gla_fwd")
        o_b, sg_b = _gla_scan(lay, proj, dw2p, db, sg0, reverse=True, name="gla_bwd")
        new_gla.append(jnp.stack([sg_f, sg_b], axis=1))
        m_a = _fused_mm("gla_out", m, GLA_V, D, tm=tm_a, tn=512,
                        rows=[(o_f, GLA_V, 0), (o_b, GLA_V, 0), (proj, GLA_V, C_G // GLA_V)],
                        vecs=[(gla_norm[l].reshape(1, 1, GLA_V), GLA_V, lambda i: 0)],
                        w=w_gla_b[l].astype(BF16), erows=[(proj, C_GA // 512)],
                        prologue=_pro_gla_out, epilogue=_epi_gate_a, out_dtype=F32)

        xbc = _conv(lay, proj, conv_w[l], conv_b[l].reshape(1, CONV_DIM))
        dtg, cumg = _dtprep(lay, proj, dt_bias[l].reshape(1, LANE), a_log[l].reshape(1, LANE))
        h0 = state_ssd[:, l].reshape(n_lat, 2, SSD_G, SSD_HPG, SSD_N, SSD_P)
        h0 = h0.transpose(0, 1, 2, 4, 3, 5).reshape(n_lat, 2, SSD_G, SSD_N, SSD_HPG * SSD_P)
        dsk = jnp.repeat(d_skip[l], SSD_P).reshape(1, D_INNER)
        y_f, hs_f = _ssd_scan(lay, xbc, dtg, cumg, h0, dsk, reverse=False, name="ssd_fwd")
        y_b, hs_b = _ssd_scan(lay, xbc, dtg, cumg, h0, dsk, reverse=True, name="ssd_bwd")
        hs = jnp.stack([hs_f, hs_b], axis=1).reshape(n_ctx, 2, SSD_G, SSD_N, SSD_HPG, SSD_P)
        new_ssd.append(hs.transpose(0, 1, 2, 4, 3, 5).reshape(n_ctx, 2, SSD_HEADS, SSD_N, SSD_P))
        merged = _fused_mm("ssd_out", m, D_INNER, D, tm=RB, tn=512,
                           rows=[(y_f, D_INNER, 0), (y_b, D_INNER, 0), (proj, D_INNER, C_Z // D_INNER)],
                           vecs=[(ssd_norm[l].reshape(1, 1, D_INNER), D_INNER, lambda i: 0)],
                           w=w_ssd_b[l].astype(BF16), erows=[(m_a, 0), (proj, C_GB // 512)],
                           prologue=_pro_ssd_out, epilogue=_epi_gate_b, out_dtype=BF16, rc=16)

        x = _fused_mm("mix_out", m, D, D, tm=tm_in, tn=512, rows=[(merged, D, 0)], vecs=[],
                      w=w_out[l].astype(BF16), erows=[(x, 0)], evecs=[(mod3, lambda i: mr(i) * 6 + 2)],
                      prologue=None, epilogue=_epi_resid, out_dtype=F32)

        h2, logits = _norm_router(lay, x, norm2[l].reshape(1, 1, D), mod3, wr)
        row_tok, row_w, block_e, n_used, pos_tk = _route(logits[:, :N_EXP], router_bias, m)
        xb = h2[row_tok]
        yb = _experts(xb, w_e1[l].astype(BF16), w_e3[l].astype(BF16), w_e2[l].astype(BF16),
                      row_w.reshape(-1, 1), block_e, n_used)
        y0 = yb[pos_tk[:, 0]]
        y1 = yb[pos_tk[:, 1]]
        if l + 1 < depth:
            x = _combine(lay, x, y0, y1, mod3)
        else:
            y_prompt, y_sample = _final(lay, x, y0, y1, mod3, norm_f.reshape(1, D))

    return (y_prompt.reshape(n_ctx, t_ctx, D), y_sample.reshape(n_lat, t_lat, D),
            jnp.stack(new_gla, axis=1), jnp.stack(new_ssd, axis=1))
```

```python
import functools
import math
from typing import NamedTuple

import jax
import jax.numpy as jnp
from jax import lax
from jax.experimental import pallas as pl
from jax.experimental.pallas import tpu as pltpu

F32 = jnp.float32
BF16 = jnp.bfloat16

D = 2048
GRID_W = 64
POS_BASE = 10000.0
EPS = 1e-6
GLA_H = 4
GLA_DK = D // 2 // GLA_H
GLA_DV = D // GLA_H
GLA_QK = GLA_H * GLA_DK
GLA_V = GLA_H * GLA_DV
GLA_LR = 16
GLA_TAU = 16.0
GLA_C = 64
D_INNER = 2 * D
SSD_P = 64
SSD_HEADS = D_INNER // SSD_P
SSD_N = 128
SSD_G = 8
SSD_HPG = SSD_HEADS // SSD_G
SSD_BC = SSD_G * SSD_N
CONV_DIM = D_INNER + 2 * SSD_BC
D_CONV = 5
SSD_C = 128
N_EXP = 16
N_GRP = 4
EPG = N_EXP // N_GRP
TOP_K = 2
D_EXP = D // 2
MOE_BLOCK = 256

LANE = 128
SUBLANE = 8
RB = 256
VMEM_LIMIT = 56 * 1024 * 1024

C_Q = 0
C_K = C_Q + GLA_QK
C_V = C_K + GLA_QK
C_G = C_V + GLA_V
C_GA = C_G + GLA_V
C_Z = C_GA + D
C_XBC = C_Z + D_INNER
C_B = C_XBC + D_INNER
C_CM = C_B + SSD_BC
C_GB = C_XBC + CONV_DIM
C_DT = C_GB + D
C_DLR = C_DT + 2 * SSD_HEADS
NP = C_DLR + 3 * LANE


class Layout(NamedTuple):
    n_ctx: int
    t_ctx: int
    n_lat: int
    t_lat: int

    @property
    def m_ctx(self):
        return self.n_ctx * self.t_ctx

    @property
    def m(self):
        return self.m_ctx + self.n_lat * self.t_lat

    @property
    def nb(self):
        return self.m // RB

    @property
    def nb_ctx(self):
        return self.m_ctx // RB

    @property
    def bps_ctx(self):
        return self.t_ctx // RB

    @property
    def bps_lat(self):
        return self.t_lat // RB


def _seq_info(lay, blk):
    is_ctx = blk < lay.nb_ctx
    lb = jnp.maximum(blk - lay.nb_ctx, 0)
    cb = jnp.minimum(blk, lay.nb_ctx - 1)
    pos = jnp.where(is_ctx, cb % lay.bps_ctx, lb % lay.bps_lat)
    bps = jnp.where(is_ctx, lay.bps_ctx, lay.bps_lat)
    return is_ctx, cb // lay.bps_ctx, jnp.minimum(lb // lay.bps_lat, lay.n_lat - 1), pos, bps


def _mod_row(lay, row_start):
    return jnp.where(row_start < lay.m_ctx, 0, 1 + jnp.maximum(row_start - lay.m_ctx, 0) // lay.t_lat)


def _cparams(sem):
    return pltpu.CompilerParams(dimension_semantics=sem, vmem_limit_bytes=VMEM_LIMIT)


def _silu(x):
    return x * jax.nn.sigmoid(x)


def _softplus(x):
    return jnp.maximum(x, 0.0) + jnp.log1p(jnp.exp(-jnp.abs(x)))


def _cumsum_rows(x, n, reverse):
    rows = lax.broadcasted_iota(jnp.int32, x.shape, 0)
    s = 1
    while s < n:
        if reverse:
            x = x + jnp.where(rows < n - s, pltpu.roll(x, n - s, 0), 0.0)
        else:
            x = x + jnp.where(rows >= s, pltpu.roll(x, s, 0), 0.0)
        s *= 2
    return x


NT = (((1,), (1,)), ((), ()))
TN = (((0,), (0,)), ((), ()))


def _embed_kernel(xp_ref, xs_ref, pos_ref, o_ref, *, lay):
    rb = pl.program_id(0)

    @pl.when(rb < lay.nb_ctx)
    def _():
        o_ref[...] = xp_ref[...]

    @pl.when(rb >= lay.nb_ctx)
    def _():
        o_ref[...] = xs_ref[...] + pos_ref[...]


def _embed(lay, xp, xs, pos):
    spec = lambda f: pl.BlockSpec((RB, D), f)
    return pl.pallas_call(
        functools.partial(_embed_kernel, lay=lay),
        out_shape=jax.ShapeDtypeStruct((lay.m, D), F32),
        grid=(lay.nb,),
        in_specs=[spec(lambda i: (jnp.minimum(i, lay.nb_ctx - 1), 0)),
                  spec(lambda i: (jnp.maximum(i - lay.nb_ctx, 0), 0)),
                  spec(lambda i: (jnp.maximum(i - lay.nb_ctx, 0) % lay.bps_lat, 0))],
        out_specs=spec(lambda i: (i, 0)),
        compiler_params=_cparams(("arbitrary",)),
        name="embed",
    )(xp, xs, pos)


def _fused_mm_kernel(*refs, n_rows, n_vecs, n_erows, n_evecs, prologue, epilogue, tm, rc):
    rows = refs[:n_rows]
    vecs = refs[n_rows:n_rows + n_vecs]
    w_ref = refs[n_rows + n_vecs]
    p = n_rows + n_vecs + 1
    erows = refs[p:p + n_erows]
    evecs = refs[p + n_erows:p + n_erows + n_evecs]
    o_ref = refs[p + n_erows + n_evecs]

    if prologue is None:
        lhs = rows[0][...]
    else:
        h_ref = refs[p + n_erows + n_evecs + 1]

        @pl.when(pl.program_id(1) == 0)
        def _():
            vs = [v[...] for v in vecs]

            def chunk(c, carry):
                sl = pl.ds(pl.multiple_of(c * rc, rc), rc)
                h_ref[sl, :] = prologue([r[sl, :] for r in rows], vs).astype(BF16)
                return carry

            lax.fori_loop(0, tm // rc, chunk, 0)

        lhs = h_ref[...]
    acc = jnp.dot(lhs, w_ref[...], preferred_element_type=F32)
    if epilogue is not None:
        acc = epilogue(acc, [e[...] for e in erows], [v[...] for v in evecs])
    o_ref[...] = acc.astype(o_ref.dtype)


def _fused_mm(name, m, k, n, *, tm, tn, rows, vecs, w, w_col0=0, erows=(), evecs=(), prologue, epilogue,
              out_dtype, rc=32):
    in_specs, args = [], []
    for arr, width, cb in rows:
        in_specs.append(pl.BlockSpec((tm, width), lambda i, j, cb=cb: (i, cb)))
        args.append(arr)
    for arr, width, fn in vecs:
        in_specs.append(pl.BlockSpec((None, 1, width), lambda i, j, fn=fn: (fn(i), 0, 0)))
        args.append(arr)
    in_specs.append(pl.BlockSpec((k, tn), lambda i, j: (0, w_col0 + j)))
    args.append(w)
    for arr, cb0 in erows:
        in_specs.append(pl.BlockSpec((tm, tn), lambda i, j, cb0=cb0: (i, cb0 + j)))
        args.append(arr)
    for arr, fn in evecs:
        in_specs.append(pl.BlockSpec((None, 1, tn), lambda i, j, fn=fn: (fn(i), 0, j)))
        args.append(arr)
    scratch = [] if prologue is None else [pltpu.VMEM((tm, k), BF16)]
    kern = functools.partial(_fused_mm_kernel, n_rows=len(rows), n_vecs=len(vecs), n_erows=len(erows),
                             n_evecs=len(evecs), prologue=prologue, epilogue=epilogue, tm=tm, rc=rc)
    return pl.pallas_call(
        kern,
        out_shape=jax.ShapeDtypeStruct((m, n), out_dtype),
        grid=(m // tm, n // tn),
        in_specs=in_specs,
        out_specs=pl.BlockSpec((tm, tn), lambda i, j: (i, j)),
        scratch_shapes=scratch,
        compiler_params=_cparams(("arbitrary", "arbitrary")),
        name=name,
    )(*args)


def _rms(x, gain):
    return x * lax.rsqrt(jnp.mean(x * x, axis=-1, keepdims=True) + EPS) * gain


def _pro_norm_mod(rows, vecs):
    (x,), (gain, scale, shift) = rows, vecs
    return _rms(x, gain) * (1.0 + scale) + shift


def _pro_silu(rows, vecs):
    return _silu(rows[0])


def _epi_bias(acc, erows, evecs):
    return acc + evecs[0]


def _epi_resid(acc, erows, evecs):
    return erows[0] + evecs[0] * acc


def _kmm_kernel(*refs, n_rows, n_vecs, n_erows, prologue, epilogue, tm, rc, nk):
    rows = refs[:n_rows]
    vecs = refs[n_rows:n_rows + n_vecs]
    w_ref = refs[n_rows + n_vecs]
    p = n_rows + n_vecs + 1
    erows = refs[p:p + n_erows]
    o_ref, acc_ref, lhs_ref, ss_ref = refs[p + n_erows:p + n_erows + 4]
    k = pl.program_id(1)

    @pl.when(k == 0)
    def _():
        acc_ref[...] = jnp.zeros_like(acc_ref)
        ss_ref[...] = jnp.zeros_like(ss_ref)

    vs = [v[...] for v in vecs]

    def chunk(c, carry):
        sl = pl.ds(pl.multiple_of(c * rc, rc), rc)
        lhs, ss = prologue([r[sl, :] for r in rows], vs)
        lhs_ref[sl, :] = lhs.astype(BF16)
        if ss is not None:
            ss_ref[sl, :] += ss
        return carry

    lax.fori_loop(0, tm // rc, chunk, 0)
    acc_ref[...] += jnp.dot(lhs_ref[...], w_ref[...], preferred_element_type=F32)

    @pl.when(k == nk - 1)
    def _():
        o_ref[...] = epilogue(acc_ref[...], ss_ref[...], [e[...] for e in erows]).astype(o_ref.dtype)


def _kmm(name, m, k, n, *, tm, tk, rows, vecs, w, erows, prologue, epilogue, out_dtype, rc=64):
    in_specs, args = [], []
    for arr, cb0 in rows:
        in_specs.append(pl.BlockSpec((tm, tk), lambda i, kk, cb0=cb0: (i, cb0 + kk)))
        args.append(arr)
    for arr in vecs:
        in_specs.append(pl.BlockSpec((1, tk), lambda i, kk: (0, kk)))
        args.append(arr)
    in_specs.append(pl.BlockSpec((tk, n), lambda i, kk: (kk, 0)))
    args.append(w)
    for arr, cb in erows:
        in_specs.append(pl.BlockSpec((tm, n), lambda i, kk, cb=cb: (i, cb)))
        args.append(arr)
    kern = functools.partial(_kmm_kernel, n_rows=len(rows), n_vecs=len(vecs), n_erows=len(erows),
                             prologue=prologue, epilogue=epilogue, tm=tm, rc=rc, nk=k // tk)
    return pl.pallas_call(
        kern,
        out_shape=jax.ShapeDtypeStruct((m, n), out_dtype),
        grid=(m // tm, k // tk),
        in_specs=in_specs,
        out_specs=pl.BlockSpec((tm, n), lambda i, kk: (i, 0)),
        scratch_shapes=[pltpu.VMEM((tm, n), F32), pltpu.VMEM((tm, tk), BF16), pltpu.VMEM((tm, 1), F32)],
        compiler_params=_cparams(("arbitrary", "arbitrary")),
        name=name,
    )(*args)


def _pro_gla_out(rows, vecs):
    (of, ob, g), (gain,) = rows, vecs
    return _rms(of + ob, gain) * _silu(g), None


def _epi_gate_a(acc, ss, erows):
    return jax.nn.sigmoid(erows[0]) * acc


def _pro_ssd_out(rows, vecs):
    (yf, yb, z), (gain,) = rows, vecs
    u = (yf + yb) * _silu(z)
    return u * gain, jnp.sum(u * u, axis=-1, keepdims=True)


def _epi_gate_b(acc, ss, erows):
    rstd = lax.rsqrt(ss * (1.0 / D_INNER) + EPS)
    return erows[0] + jax.nn.sigmoid(erows[1]) * (acc * rstd)


def _gla_kernel(q_ref, k_ref, v_ref, dlr_ref, dw_ref, db_ref, s0_ref, o_ref, sf_ref, s_ref, *, lay, reverse):
    i = pl.program_id(1)
    blk = (lay.nb - 1 - i) if reverse else i
    is_ctx, _, _, pos, bps = _seq_info(lay, blk)
    first = (pos == bps - 1) if reverse else (pos == 0)
    last = (pos == 0) if reverse else (pos == bps - 1)

    @pl.when(first & is_ctx)
    def _():
        s_ref[...] = jnp.zeros_like(s_ref)

    @pl.when(first & jnp.logical_not(is_ctx))
    def _():
        s_ref[...] = s0_ref[...]

    x = jnp.dot(dlr_ref[...].astype(BF16), dw_ref[...], preferred_element_type=F32) + db_ref[...]
    log_a = (jnp.minimum(x, 0.0) - jnp.log1p(jnp.exp(-jnp.abs(x)))) * (1.0 / GLA_TAU)

    r = lax.broadcasted_iota(jnp.int32, (GLA_C, GLA_C), 0)
    c = lax.broadcasted_iota(jnp.int32, (GLA_C, GLA_C), 1)
    mask = (c >= r) if reverse else (c <= r)
    n_chunks = RB // GLA_C
    order = range(n_chunks - 1, -1, -1) if reverse else range(n_chunks)
    for ci in order:
        sl = slice(ci * GLA_C, (ci + 1) * GLA_C)
        b = _cumsum_rows(log_a[sl], GLA_C, reverse)
        q = q_ref[sl, :] * (GLA_DK ** -0.5)
        k = k_ref[sl, :]
        v = v_ref[sl, :].astype(BF16)
        qe = (q * jnp.exp(b)).astype(BF16)
        ke = (k * jnp.exp(-b)).astype(BF16)
        att = jnp.where(mask, lax.dot_general(qe, ke, NT, preferred_element_type=F32), 0.0)
        s = s_ref[...]
        o = jnp.dot(att.astype(BF16), v, preferred_element_type=F32)
        o = o + jnp.dot(qe, s.astype(BF16), preferred_element_type=F32)
        o_ref[sl, :] = o
        b_end = b[0:1, :] if reverse else b[GLA_C - 1:GLA_C, :]
        kd = (k * jnp.exp(b_end - b)).astype(BF16)
        e_col = jnp.exp(jnp.transpose(jnp.broadcast_to(b_end, (LANE, GLA_DK))))
        upd = lax.dot_general(kd, v, TN, preferred_element_type=F32)
        s_ref[...] = s * jnp.concatenate([e_col] * (GLA_DV // LANE), axis=1) + upd

    @pl.when(last & is_ctx)
    def _():
        sf_ref[...] = s_ref[...]


def _gla_scan(lay, proj, dw2p, db, s0, layer, *, reverse, name):
    d = 1 if reverse else 0
    blk = (lambda i: lay.nb - 1 - i) if reverse else (lambda i: i)
    cseq = lambda i: _seq_info(lay, blk(i))[1]
    lseq = lambda i: _seq_info(lay, blk(i))[2]
    return pl.pallas_call(
        functools.partial(_gla_kernel, lay=lay, reverse=reverse),
        out_shape=(jax.ShapeDtypeStruct((lay.m, GLA_V), F32),
                   jax.ShapeDtypeStruct((lay.n_ctx, GLA_H, GLA_DK, GLA_DV), F32)),
        grid=(GLA_H, lay.nb),
        in_specs=[
            pl.BlockSpec((RB, GLA_DK), lambda h, i: (blk(i), C_Q // GLA_DK + h)),
            pl.BlockSpec((RB, GLA_DK), lambda h, i: (blk(i), C_K // GLA_DK + h)),
            pl.BlockSpec((RB, GLA_DV), lambda h, i: (blk(i), C_V // GLA_DV + h)),
            pl.BlockSpec((RB, LANE), lambda h, i: (blk(i), C_DLR // LANE + d)),
            pl.BlockSpec((None, LANE, GLA_DK), lambda h, i: (d, 0, h)),
            pl.BlockSpec((None, 1, GLA_DK), lambda h, i: (d, 0, h)),
            pl.BlockSpec((None, None, None, None, GLA_DK, GLA_DV), lambda h, i: (lseq(i), layer, d, h, 0, 0)),
        ],
        out_specs=(pl.BlockSpec((RB, GLA_DV), lambda h, i: (blk(i), h)),
                   pl.BlockSpec((None, None, GLA_DK, GLA_DV), lambda h, i: (cseq(i), h, 0, 0))),
        scratch_shapes=[pltpu.VMEM((GLA_DK, GLA_DV), F32)],
        compiler_params=_cparams(("arbitrary", "arbitrary")),
        name=name,
    )(proj, proj, proj, proj, dw2p, db, s0)


CONV_TC = 2048
CONV_SC = 512


def _conv_kernel(prev_ref, cur_ref, next_ref, w_ref, b_ref, o_ref, *, lay):
    rb = pl.program_id(0)
    _, _, _, pos, bps = _seq_info(lay, rb)
    n = RB + 2 * SUBLANE
    for s in range(CONV_TC // CONV_SC):
        cs = slice(s * CONV_SC, (s + 1) * CONV_SC)
        prev = jnp.where(pos > 0, prev_ref[:, cs], 0.0)
        nxt = jnp.where(pos < bps - 1, next_ref[:, cs], 0.0)
        ext = jnp.concatenate([prev, cur_ref[:, cs], nxt], axis=0)
        w = w_ref[:, cs]
        acc = jnp.broadcast_to(b_ref[:, cs], (RB, CONV_SC))
        for j in range(D_CONV):
            off = j - D_CONV // 2
            sh = ext if off == 0 else pltpu.roll(ext, (-off) % n, 0)
            acc = acc + w[j:j + 1, :] * sh[SUBLANE:SUBLANE + RB, :]
        o_ref[:, cs] = _silu(acc)


def _conv(lay, proj, cw, cb):
    c0 = C_XBC // CONV_TC
    rpb = RB // SUBLANE
    return pl.pallas_call(
        functools.partial(_conv_kernel, lay=lay),
        out_shape=jax.ShapeDtypeStruct((lay.m, CONV_DIM), F32),
        grid=(lay.nb, CONV_DIM // CONV_TC),
        in_specs=[
            pl.BlockSpec((SUBLANE, CONV_TC), lambda i, j: (jnp.maximum(i * rpb - 1, 0), c0 + j)),
            pl.BlockSpec((RB, CONV_TC), lambda i, j: (i, c0 + j)),
            pl.BlockSpec((SUBLANE, CONV_TC), lambda i, j: (jnp.minimum((i + 1) * rpb, lay.m // SUBLANE - 1), c0 + j)),
            pl.BlockSpec((D_CONV, CONV_TC), lambda i, j: (0, j)),
            pl.BlockSpec((1, CONV_TC), lambda i, j: (0, j)),
        ],
        out_specs=pl.BlockSpec((RB, CONV_TC), lambda i, j: (i, j)),
        compiler_params=_cparams(("arbitrary", "arbitrary")),
        name="conv",
    )(proj, proj, proj, cw, cb)


def _dtprep_kernel(raw_ref, bias_ref, alog_ref, dt_ref, cum_ref):
    dt = _softplus(raw_ref[...] + bias_ref[...])
    dta = dt * (-jnp.exp(alog_ref[...]))
    lane = lax.broadcasted_iota(jnp.int32, (SSD_C, LANE), 1)
    cums = []
    for ci in range(RB // SSD_C):
        t = dta[ci * SSD_C:(ci + 1) * SSD_C]
        cums.append(jnp.where(lane < SSD_HEADS, _cumsum_rows(t, SSD_C, False), _cumsum_rows(t, SSD_C, True)))
    cum = jnp.concatenate(cums, axis=0)
    for d in range(2):
        for g in range(SSD_G):
            base = d * SSD_HEADS + g * SSD_HPG
            sh = (LANE - base) % LANE
            dt_ref[d, g] = dt if sh == 0 else pltpu.roll(dt, sh, 1)
            cum_ref[d, g] = cum if sh == 0 else pltpu.roll(cum, sh, 1)


def _dtprep(lay, proj, dt_bias, a_log):
    out = jax.ShapeDtypeStruct((2, SSD_G, lay.m, LANE), F32)
    ospec = pl.BlockSpec((2, SSD_G, RB, LANE), lambda i: (0, 0, i, 0))
    vspec = pl.BlockSpec((1, LANE), lambda i: (0, 0))
    return pl.pallas_call(
        _dtprep_kernel,
        out_shape=(out, out),
        grid=(lay.nb,),
        in_specs=[pl.BlockSpec((RB, LANE), lambda i: (i, C_DT // LANE)), vspec, vspec],
        out_specs=(ospec, ospec),
        compiler_params=_cparams(("arbitrary",)),
        name="dtprep",
    )(proj, dt_bias, a_log)


def _ssd_kernel(x_ref, b_ref, c_ref, dt_ref, cum_ref, h0_ref, dsk_ref, y_ref, hf_ref, h_ref, *, lay, reverse):
    i = pl.program_id(1)
    blk = (lay.nb - 1 - i) if reverse else i
    is_ctx, _, _, pos, bps = _seq_info(lay, blk)
    first = (pos == bps - 1) if reverse else (pos == 0)
    last = (pos == 0) if reverse else (pos == bps - 1)

    @pl.when(first & is_ctx)
    def _():
        h_ref[...] = jnp.zeros_like(h_ref)

    @pl.when(first & jnp.logical_not(is_ctx))
    def _():
        h_ref[...] = h0_ref[...]

    r = lax.broadcasted_iota(jnp.int32, (SSD_C, SSD_C), 0)
    c = lax.broadcasted_iota(jnp.int32, (SSD_C, SSD_C), 1)
    mask = (c >= r) if reverse else (c <= r)
    lo = c < SSD_P
    n_chunks = RB // SSD_C
    order = range(n_chunks - 1, -1, -1) if reverse else range(n_chunks)
    n_pairs = SSD_HPG // 2
    for ci in order:
        sl = slice(ci * SSD_C, (ci + 1) * SSD_C)
        x = x_ref[sl, :]
        bm = b_ref[sl, :].astype(BF16)
        cm = c_ref[sl, :].astype(BF16)
        dt = dt_ref[sl, :]
        cum = cum_ref[sl, :]
        cum_t = jnp.transpose(cum)
        cb = lax.dot_general(cm, bm, NT, preferred_element_type=F32)
        h = h_ref[...]
        y_inter = jnp.dot(cm, h.astype(BF16), preferred_element_type=F32)
        e_idx = 0 if reverse else SSD_C - 1
        ys, xdes, decs = [], [], []
        for j in range(n_pairs):
            cols, ls = [], []
            for rr in (2 * j, 2 * j + 1):
                col = jnp.broadcast_to(cum[:, rr:rr + 1], (SSD_C, SSD_C))
                seg = col - cum_t[rr:rr + 1, :]
                ls.append((jnp.exp(jnp.where(mask, seg, -1e30)) * cb).astype(BF16))
                cols.append(col)
            cp = jnp.where(lo, cols[0], cols[1])
            dtp = jnp.where(lo, jnp.broadcast_to(dt[:, 2 * j:2 * j + 1], (SSD_C, SSD_C)),
                            jnp.broadcast_to(dt[:, 2 * j + 1:2 * j + 2], (SSD_C, SSD_C)))
            xp = x[:, j * LANE:(j + 1) * LANE]
            xdt = xp * dtp
            rhs = jnp.concatenate([jnp.where(lo, xdt, 0.0), jnp.where(lo, 0.0, xdt)], axis=0).astype(BF16)
            lhs = jnp.concatenate(ls, axis=1)
            y = jnp.dot(lhs, rhs, preferred_element_type=F32)
            y = y + jnp.exp(cp) * y_inter[:, j * LANE:(j + 1) * LANE]
            if not reverse:
                y = y + dsk_ref[:, j * LANE:(j + 1) * LANE] * xp
            ys.append(y)
            edge = cp[e_idx:e_idx + 1, :]
            xdes.append((xdt * jnp.exp(edge - cp)).astype(BF16))
            decs.append(jnp.exp(edge))
        y_ref[sl, :] = jnp.concatenate(ys, axis=1)
        upd = lax.dot_general(bm, jnp.concatenate(xdes, axis=1), TN, preferred_element_type=F32)
        h_ref[...] = h * jnp.concatenate(decs, axis=1) + upd

    @pl.when(last & is_ctx)
    def _():
        hf_ref[...] = h_ref[...]


def _ssd_scan(lay, xbc, dtg, cumg, h0, dsk, *, reverse, name):
    d = 1 if reverse else 0
    blk = (lambda i: lay.nb - 1 - i) if reverse else (lambda i: i)
    cseq = lambda i: _seq_info(lay, blk(i))[1]
    lseq = lambda i: _seq_info(lay, blk(i))[2]
    gw = SSD_HPG * SSD_P
    return pl.pallas_call(
        functools.partial(_ssd_kernel, lay=lay, reverse=reverse),
        out_shape=(jax.ShapeDtypeStruct((lay.m, D_INNER), F32),
                   jax.ShapeDtypeStruct((lay.n_ctx, SSD_G, SSD_N, gw), F32)),
        grid=(SSD_G, lay.nb),
        in_specs=[
            pl.BlockSpec((RB, gw), lambda g, i: (blk(i), g)),
            pl.BlockSpec((RB, SSD_N), lambda g, i: (blk(i), D_INNER // SSD_N + g)),
            pl.BlockSpec((RB, SSD_N), lambda g, i: (blk(i), (D_INNER + SSD_BC) // SSD_N + g)),
            pl.BlockSpec((None, None, RB, LANE), lambda g, i: (d, g, blk(i), 0)),
            pl.BlockSpec((None, None, RB, LANE), lambda g, i: (d, g, blk(i), 0)),
            pl.BlockSpec((None, None, None, SSD_N, gw), lambda g, i: (lseq(i), d, g, 0, 0)),
            pl.BlockSpec((1, gw), lambda g, i: (0, g)),
        ],
        out_specs=(pl.BlockSpec((RB, gw), lambda g, i: (blk(i), g)),
                   pl.BlockSpec((None, None, SSD_N, gw), lambda g, i: (cseq(i), g, 0, 0))),
        scratch_shapes=[pltpu.VMEM((SSD_N, gw), F32)],
        compiler_params=_cparams(("arbitrary", "arbitrary")),
        name=name,
    )(xbc, xbc, xbc, dtg, cumg, h0, dsk)


def _split3(x):
    hi = x.astype(BF16)
    r1 = x - hi.astype(F32)
    mid = r1.astype(BF16)
    lo = (r1 - mid.astype(F32)).astype(BF16)
    return hi, mid, lo


def _norm_router_kernel(x_ref, gain_ref, scale_ref, shift_ref, wr_ref, h_ref, lg_ref):
    h = _rms(x_ref[...], gain_ref[...]) * (1.0 + scale_ref[...]) + shift_ref[...]
    h_ref[...] = h.astype(BF16)
    hs = _split3(h)
    ws = _split3(wr_ref[...])
    acc = jnp.zeros(lg_ref.shape, F32)
    for a, b in ((2, 0), (0, 2), (1, 1), (1, 0), (0, 1), (0, 0)):
        acc = acc + jnp.dot(hs[a], ws[b], preferred_element_type=F32)
    lg_ref[...] = acc


def _norm_router(lay, x, gain, mod3, wr):
    mrow = lambda i: _mod_row(lay, i * RB)
    vspec = lambda fn: pl.BlockSpec((None, 1, D), lambda i: (fn(i), 0, 0))
    return pl.pallas_call(
        _norm_router_kernel,
        out_shape=(jax.ShapeDtypeStruct((lay.m, D), BF16), jax.ShapeDtypeStruct((lay.m, LANE), F32)),
        grid=(lay.nb,),
        in_specs=[pl.BlockSpec((RB, D), lambda i: (i, 0)), vspec(lambda i: 0),
                  vspec(lambda i: mrow(i) * 6 + 4), vspec(lambda i: mrow(i) * 6 + 3),
                  pl.BlockSpec((D, LANE), lambda i: (0, 0))],
        out_specs=(pl.BlockSpec((RB, D), lambda i: (i, 0)), pl.BlockSpec((RB, LANE), lambda i: (i, 0))),
        compiler_params=_cparams(("arbitrary",)),
        name="norm_router",
    )(x, gain, mod3, mod3, wr)


def _expert_kernel(be_ref, nu_ref, x_ref, w1_ref, w3_ref, w2_ref, o_ref):
    b = pl.program_id(0)

    @pl.when(b < nu_ref[0])
    def _():
        x = x_ref[...]
        h1 = jnp.dot(x, w1_ref[...], preferred_element_type=F32)
        h3 = jnp.dot(x, w3_ref[...], preferred_element_type=F32)
        hid = (_silu(h1) * h3).astype(BF16)
        o_ref[...] = jnp.dot(hid, w2_ref[...], preferred_element_type=F32)

    @pl.when(b >= nu_ref[0])
    def _():
        o_ref[...] = jnp.zeros_like(o_ref)


def _experts(xb, w1, w3, w2, block_e, n_used):
    n_rows = xb.shape[0]
    n_blocks = n_rows // MOE_BLOCK
    wspec = lambda a, b: pl.BlockSpec((None, a, b), lambda i, be, nu: (be[i], 0, 0))
    return pl.pallas_call(
        _expert_kernel,
        out_shape=jax.ShapeDtypeStruct((n_rows, D), F32),
        grid_spec=pltpu.PrefetchScalarGridSpec(
            num_scalar_prefetch=2,
            grid=(n_blocks,),
            in_specs=[pl.BlockSpec((MOE_BLOCK, D), lambda i, be, nu: (i, 0)),
                      wspec(D, D_EXP), wspec(D, D_EXP), wspec(D_EXP, D)],
            out_specs=pl.BlockSpec((MOE_BLOCK, D), lambda i, be, nu: (i, 0)),
        ),
        compiler_params=_cparams(("arbitrary",)),
        name="experts",
    )(block_e, n_used, xb, w1, w3, w2)


def _moe_residual(x_ref, y0_ref, y1_ref, tw_ref, gate_ref):
    tw = tw_ref[...]
    return x_ref[...] + gate_ref[...] * (y0_ref[...] * tw[:, 0:1] + y1_ref[...] * tw[:, 1:2])


def _combine_kernel(x_ref, y0_ref, y1_ref, tw_ref, gate_ref, o_ref):
    o_ref[...] = _moe_residual(x_ref, y0_ref, y1_ref, tw_ref, gate_ref)


def _combine(lay, x, y0, y1, tw, mod3):
    mrow = lambda i: _mod_row(lay, i * RB)
    spec = pl.BlockSpec((RB, D), lambda i: (i, 0))
    return pl.pallas_call(
        _combine_kernel,
        out_shape=jax.ShapeDtypeStruct((lay.m, D), F32),
        grid=(lay.nb,),
        in_specs=[spec, spec, spec, pl.BlockSpec((RB, TOP_K), lambda i: (i, 0)),
                  pl.BlockSpec((None, 1, D), lambda i: (mrow(i) * 6 + 5, 0, 0))],
        out_specs=spec,
        compiler_params=_cparams(("arbitrary",)),
        name="combine",
    )(x, y0, y1, tw, mod3)


def _final_kernel(x_ref, y0_ref, y1_ref, tw_ref, gate_ref, nf_ref, op_ref, os_ref, *, lay):
    rb = pl.program_id(0)
    y = _rms(_moe_residual(x_ref, y0_ref, y1_ref, tw_ref, gate_ref), nf_ref[...])

    @pl.when(rb < lay.nb_ctx)
    def _():
        op_ref[...] = y

    @pl.when(rb >= lay.nb_ctx)
    def _():
        os_ref[...] = y


def _final(lay, x, y0, y1, tw, mod3, norm_f):
    mrow = lambda i: _mod_row(lay, i * RB)
    spec = pl.BlockSpec((RB, D), lambda i: (i, 0))
    return pl.pallas_call(
        functools.partial(_final_kernel, lay=lay),
        out_shape=(jax.ShapeDtypeStruct((lay.m_ctx, D), F32), jax.ShapeDtypeStruct((lay.m - lay.m_ctx, D), F32)),
        grid=(lay.nb,),
        in_specs=[spec, spec, spec, pl.BlockSpec((RB, TOP_K), lambda i: (i, 0)),
                  pl.BlockSpec((None, 1, D), lambda i: (mrow(i) * 6 + 5, 0, 0)),
                  pl.BlockSpec((1, D), lambda i: (0, 0))],
        out_specs=(pl.BlockSpec((RB, D), lambda i: (jnp.minimum(i, lay.nb_ctx - 1), 0)),
                   pl.BlockSpec((RB, D), lambda i: (jnp.maximum(i - lay.nb_ctx, 0), 0))),
        compiler_params=_cparams(("arbitrary",)),
        name="final",
    )(x, y0, y1, tw, mod3, norm_f)


def _route(logits, router_bias, n_tok):
    n_asg = n_tok * TOP_K
    scores = jax.nn.sigmoid(logits)
    biased = scores + router_bias.astype(F32)
    group_score = lax.top_k(biased.reshape(n_tok, N_GRP, EPG), TOP_K)[0].sum(-1)
    best_group = jnp.argmax(group_score, axis=-1)
    expert_group = jnp.arange(N_EXP) // EPG
    masked = jnp.where(expert_group[None, :] == best_group[:, None], biased, -jnp.inf)
    _, top_idx = lax.top_k(masked, TOP_K)
    top_w = jnp.take_along_axis(scores, top_idx, axis=-1)
    top_w = top_w / jnp.sum(top_w, axis=-1, keepdims=True)
    e_flat = top_idx.reshape(-1)
    tok_flat = jnp.arange(n_asg, dtype=jnp.int32) // TOP_K
    cb = MOE_BLOCK
    oh = (e_flat[:, None] == jnp.arange(N_EXP, dtype=e_flat.dtype)[None, :]).astype(F32).reshape(n_asg // cb, cb, N_EXP)
    tri = (jnp.arange(cb)[:, None] > jnp.arange(cb)[None, :]).astype(F32)
    within = jnp.einsum('ts,bse->bte', tri, oh)
    bsum = oh.sum(1)
    counts = bsum.sum(0).astype(jnp.int32)
    padded = (counts + MOE_BLOCK - 1) // MOE_BLOCK * MOE_BLOCK
    padded_end = jnp.cumsum(padded)
    padded_start = padded_end - padded
    base = jnp.cumsum(bsum, axis=0) - bsum + padded_start.astype(F32)[None, :]
    dest = ((within + base[:, None, :]) * oh).sum(-1).reshape(n_asg).astype(jnp.int32)
    n_blocks = -(-n_asg // MOE_BLOCK) + N_EXP
    n_rows = n_blocks * MOE_BLOCK
    row_tok = jnp.zeros((n_rows,), jnp.int32).at[dest].set(tok_flat)
    block_start = jnp.arange(n_blocks, dtype=jnp.int32) * MOE_BLOCK
    block_e = jnp.minimum(jnp.searchsorted(padded_end, block_start, side='right'), N_EXP - 1).astype(jnp.int32)
    n_used = (padded_end[-1] // MOE_BLOCK).astype(jnp.int32).reshape(1)
    return row_tok, top_w, block_e, n_used, dest.reshape(n_tok, TOP_K)


def _sincos_2d(rows, cols, dim):
    quarter = dim // 4
    omega = 1.0 / (POS_BASE ** (jnp.arange(quarter, dtype=F32) / quarter))

    def axis_embed(n):
        ang = jnp.arange(n, dtype=F32)[:, None] * omega[None, :]
        return jnp.concatenate([jnp.sin(ang), jnp.cos(ang)], axis=-1)

    er = jnp.broadcast_to(axis_embed(rows)[:, None, :], (rows, cols, dim // 2))
    ec = jnp.broadcast_to(axis_embed(cols)[None, :, :], (rows, cols, dim // 2))
    return jnp.concatenate([er, ec], axis=-1).reshape(rows * cols, dim)


def _rearranged_w_in(w):
    sizes = (GLA_QK, GLA_QK, GLA_V, GLA_V, 2 * GLA_LR, D_INNER, CONV_DIM, 2 * SSD_HEADS, D, D)
    offs = [0]
    for s in sizes:
        offs.append(offs[-1] + s)
    q, k, v, g, dlr, z, xbc, dt, ga, gb = [w[:, offs[i]:offs[i + 1]] for i in range(len(sizes))]
    zpad = jnp.zeros((w.shape[0], LANE - GLA_LR), w.dtype)
    return jnp.concatenate([q, k, v, g, ga, z, xbc, gb, dt, dlr[:, :GLA_LR], zpad, dlr[:, GLA_LR:], zpad,
                            jnp.zeros((w.shape[0], LANE), w.dtype)], axis=1).astype(BF16)


def _pick_tile(lay, want):
    t = want
    while lay.m_ctx % t or lay.t_lat % t:
        t //= 2
    return t


def kernel(x_prompt, x_sample, c, state_gla, state_ssd, c_ctx, w_ada, b_ada, norm1, norm2, w_in, gla_dw2, gla_db, gla_norm, w_gla_b, conv_w, conv_b, a_log, dt_bias, d_skip, ssd_norm, w_ssd_b, w_out, w_router, router_bias, w_e1, w_e3, w_e2, norm_f):
    n_ctx, t_ctx, _ = x_prompt.shape
    n_lat, t_lat, _ = x_sample.shape
    lay = Layout(n_ctx, t_ctx, n_lat, t_lat)
    depth = w_in.shape[0]
    m = lay.m
    assert t_ctx % RB == 0 and t_lat % RB == 0 and t_lat % GRID_W == 0

    pos = _sincos_2d(t_lat // GRID_W, GRID_W, D)
    x = _embed(lay, x_prompt.reshape(lay.m_ctx, D), x_sample.reshape(n_lat * t_lat, D), pos)

    n_mod = 16
    cond = jnp.concatenate([c_ctx[None, :], c, jnp.zeros((n_mod - 1 - n_lat, D), F32)], axis=0)
    wr = jnp.pad(w_router, ((0, 0), (0, LANE - N_EXP)))

    tm_in = _pick_tile(lay, 1024)
    tm_a = _pick_tile(lay, 512)
    mrow = lambda tm: (lambda i: _mod_row(lay, i * tm))
    new_gla, new_ssd = [], []
    y_prompt = y_sample = None
    for l in range(depth):
        mod = _fused_mm("ada", n_mod, D, 6 * D, tm=n_mod, tn=1024, rows=[(cond, D, 0)], vecs=[],
                        w=w_ada[l].astype(BF16), evecs=[(b_ada[l].reshape(1, 1, 6 * D), lambda i: 0)],
                        prologue=_pro_silu, epilogue=_epi_bias, out_dtype=F32, rc=n_mod)
        mod3 = mod.reshape(n_mod * 6, 1, D)
        n1 = norm1[l].reshape(1, 1, D)
        mr = mrow(tm_in)
        proj = _fused_mm("in_proj", m, D, NP, tm=tm_in, tn=512, rows=[(x, D, 0)],
                         vecs=[(n1, D, lambda i: 0), (mod3, D, lambda i: mr(i) * 6 + 1),
                               (mod3, D, lambda i: mr(i) * 6 + 0)],
                         w=_rearranged_w_in(w_in[l]), prologue=_pro_norm_mod, epilogue=None, out_dtype=F32)

        dw2p = jnp.pad(gla_dw2[l], ((0, 0), (0, LANE - GLA_LR), (0, 0))).astype(BF16)
        db = gla_db[l].reshape(2, 1, GLA_QK)
        o_f, sg_f = _gla_scan(lay, proj, dw2p, db, state_gla, l, reverse=False, name="gla_fwd")
        o_b, sg_b = _gla_scan(lay, proj, dw2p, db, state_gla, l, reverse=True, name="gla_bwd")
        new_gla.append(jnp.stack([sg_f, sg_b], axis=1))
        m_a = _kmm("gla_out", m, GLA_V, D, tm=tm_a, tk=GLA_DV,
                   rows=[(o_f, 0), (o_b, 0), (proj, C_G // GLA_DV)], vecs=[gla_norm[l].reshape(1, GLA_V)],
                   w=w_gla_b[l].astype(BF16), erows=[(proj, C_GA // D)],
                   prologue=_pro_gla_out, epilogue=_epi_gate_a, out_dtype=F32)

        xbc = _conv(lay, proj, conv_w[l], conv_b[l].reshape(1, CONV_DIM))
        dtg, cumg = _dtprep(lay, proj, dt_bias[l].reshape(1, LANE), a_log[l].reshape(1, LANE))
        h0 = state_ssd[:, l].reshape(n_lat, 2, SSD_G, SSD_HPG, SSD_N, SSD_P)
        h0 = h0.transpose(0, 1, 2, 4, 3, 5).reshape(n_lat, 2, SSD_G, SSD_N, SSD_HPG * SSD_P)
        dsk = jnp.repeat(d_skip[l], SSD_P).reshape(1, D_INNER)
        y_f, hs_f = _ssd_scan(lay, xbc, dtg, cumg, h0, dsk, reverse=False, name="ssd_fwd")
        y_b, hs_b = _ssd_scan(lay, xbc, dtg, cumg, h0, dsk, reverse=True, name="ssd_bwd")
        hs = jnp.stack([hs_f, hs_b], axis=1).reshape(n_ctx, 2, SSD_G, SSD_N, SSD_HPG, SSD_P)
        new_ssd.append(hs.transpose(0, 1, 2, 4, 3, 5).reshape(n_ctx, 2, SSD_HEADS, SSD_N, SSD_P))
        merged = _kmm("ssd_out", m, D_INNER, D, tm=tm_a, tk=512,
                      rows=[(y_f, 0), (y_b, 0), (proj, C_Z // 512)], vecs=[ssd_norm[l].reshape(1, D_INNER)],
                      w=w_ssd_b[l].astype(BF16), erows=[(m_a, 0), (proj, C_GB // D)],
                      prologue=_pro_ssd_out, epilogue=_epi_gate_b, out_dtype=BF16)

        x = _fused_mm("mix_out", m, D, D, tm=tm_in, tn=512, rows=[(merged, D, 0)], vecs=[],
                      w=w_out[l].astype(BF16), erows=[(x, 0)], evecs=[(mod3, lambda i: mr(i) * 6 + 2)],
                      prologue=None, epilogue=_epi_resid, out_dtype=F32)

        h2, logits = _norm_router(lay, x, norm2[l].reshape(1, 1, D), mod3, wr)
        row_tok, top_w, block_e, n_used, pos_tk = _route(logits[:, :N_EXP], router_bias, m)
        xb = h2[row_tok]
        yb = _experts(xb, w_e1[l].astype(BF16), w_e3[l].astype(BF16), w_e2[l].astype(BF16), block_e, n_used)
        y0 = yb[pos_tk[:, 0]]
        y1 = yb[pos_tk[:, 1]]
        if l + 1 < depth:
            x = _combine(lay, x, y0, y1, top_w, mod3)
        else:
            y_prompt, y_sample = _final(lay, x, y0, y1, top_w, mod3, norm_f.reshape(1, D))

    return (y_prompt.reshape(n_ctx, t_ctx, D), y_sample.reshape(n_lat, t_lat, D),
            jnp.stack(new_gla, axis=1), jnp.stack(new_ssd, axis=1))
```

```python
import functools
import math
from typing import NamedTuple

import jax
import jax.numpy as jnp
from jax import lax
from jax.experimental import pallas as pl
from jax.experimental.pallas import tpu as pltpu

F32 = jnp.float32
BF16 = jnp.bfloat16

D = 2048
GRID_W = 64
POS_BASE = 10000.0
EPS = 1e-6
GLA_H = 4
GLA_DK = D // 2 // GLA_H
GLA_DV = D // GLA_H
GLA_QK = GLA_H * GLA_DK
GLA_V = GLA_H * GLA_DV
GLA_LR = 16
GLA_TAU = 16.0
GLA_C = 64
D_INNER = 2 * D
SSD_P = 64
SSD_HEADS = D_INNER // SSD_P
SSD_N = 128
SSD_G = 8
SSD_HPG = SSD_HEADS // SSD_G
SSD_BC = SSD_G * SSD_N
CONV_DIM = D_INNER + 2 * SSD_BC
D_CONV = 5
SSD_C = 128
N_EXP = 16
N_GRP = 4
EPG = N_EXP // N_GRP
TOP_K = 2
D_EXP = D // 2
MOE_BLOCK = 256

LANE = 128
SUBLANE = 8
RB = 256
VMEM_LIMIT = 56 * 1024 * 1024

C_Q = 0
C_K = C_Q + GLA_QK
C_V = C_K + GLA_QK
C_G = C_V + GLA_V
C_GA = C_G + GLA_V
C_Z = C_GA + D
C_XBC = C_Z + D_INNER
C_B = C_XBC + D_INNER
C_CM = C_B + SSD_BC
C_GB = C_XBC + CONV_DIM
C_DT = C_GB + D
C_DLR = C_DT + 2 * SSD_HEADS
NP = C_DLR + 3 * LANE


class Layout(NamedTuple):
    n_ctx: int
    t_ctx: int
    n_lat: int
    t_lat: int

    @property
    def m_ctx(self):
        return self.n_ctx * self.t_ctx

    @property
    def m(self):
        return self.m_ctx + self.n_lat * self.t_lat

    @property
    def nb(self):
        return self.m // RB

    @property
    def nb_ctx(self):
        return self.m_ctx // RB

    @property
    def bps_ctx(self):
        return self.t_ctx // RB

    @property
    def bps_lat(self):
        return self.t_lat // RB


def _seq_info(lay, blk):
    is_ctx = blk < lay.nb_ctx
    lb = jnp.maximum(blk - lay.nb_ctx, 0)
    cb = jnp.minimum(blk, lay.nb_ctx - 1)
    pos = jnp.where(is_ctx, cb % lay.bps_ctx, lb % lay.bps_lat)
    bps = jnp.where(is_ctx, lay.bps_ctx, lay.bps_lat)
    return is_ctx, cb // lay.bps_ctx, jnp.minimum(lb // lay.bps_lat, lay.n_lat - 1), pos, bps


def _mod_row(lay, row_start):
    return jnp.where(row_start < lay.m_ctx, 0, 1 + jnp.maximum(row_start - lay.m_ctx, 0) // lay.t_lat)


def _cparams(sem):
    return pltpu.CompilerParams(dimension_semantics=sem, vmem_limit_bytes=VMEM_LIMIT)


def _silu(x):
    return x * jax.nn.sigmoid(x)


def _softplus(x):
    return jnp.maximum(x, 0.0) + jnp.log1p(jnp.exp(-jnp.abs(x)))


def _cumsum_rows(x, n, reverse):
    rows = lax.broadcasted_iota(jnp.int32, x.shape, 0)
    s = 1
    while s < n:
        if reverse:
            x = x + jnp.where(rows < n - s, pltpu.roll(x, n - s, 0), 0.0)
        else:
            x = x + jnp.where(rows >= s, pltpu.roll(x, s, 0), 0.0)
        s *= 2
    return x


NT = (((1,), (1,)), ((), ()))
TN = (((0,), (0,)), ((), ()))


def _embed_kernel(xp_ref, xs_ref, pos_ref, o_ref, *, lay):
    rb = pl.program_id(0)

    @pl.when(rb < lay.nb_ctx)
    def _():
        o_ref[...] = xp_ref[...]

    @pl.when(rb >= lay.nb_ctx)
    def _():
        o_ref[...] = xs_ref[...] + pos_ref[...]


def _embed(lay, xp, xs, pos):
    spec = lambda f: pl.BlockSpec((RB, D), f)
    return pl.pallas_call(
        functools.partial(_embed_kernel, lay=lay),
        out_shape=jax.ShapeDtypeStruct((lay.m, D), F32),
        grid=(lay.nb,),
        in_specs=[spec(lambda i: (jnp.minimum(i, lay.nb_ctx - 1), 0)),
                  spec(lambda i: (jnp.maximum(i - lay.nb_ctx, 0), 0)),
                  spec(lambda i: (jnp.maximum(i - lay.nb_ctx, 0) % lay.bps_lat, 0))],
        out_specs=spec(lambda i: (i, 0)),
        compiler_params=_cparams(("arbitrary",)),
        name="embed",
    )(xp, xs, pos)


def _fused_mm_kernel(*refs, n_rows, n_vecs, n_erows, n_evecs, prologue, epilogue, tm, rc):
    rows = refs[:n_rows]
    vecs = refs[n_rows:n_rows + n_vecs]
    w_ref = refs[n_rows + n_vecs]
    p = n_rows + n_vecs + 1
    erows = refs[p:p + n_erows]
    evecs = refs[p + n_erows:p + n_erows + n_evecs]
    o_ref = refs[p + n_erows + n_evecs]

    if prologue is None:
        lhs = rows[0][...]
    else:
        h_ref = refs[p + n_erows + n_evecs + 1]

        @pl.when(pl.program_id(1) == 0)
        def _():
            vs = [v[...] for v in vecs]

            def chunk(c, carry):
                sl = pl.ds(pl.multiple_of(c * rc, rc), rc)
                h_ref[sl, :] = prologue([r[sl, :] for r in rows], vs).astype(BF16)
                return carry

            lax.fori_loop(0, tm // rc, chunk, 0)

        lhs = h_ref[...]
    acc = jnp.dot(lhs, w_ref[...], preferred_element_type=F32)
    if epilogue is not None:
        acc = epilogue(acc, [e[...] for e in erows], [v[...] for v in evecs])
    o_ref[...] = acc.astype(o_ref.dtype)


def _fused_mm(name, m, k, n, *, tm, tn, rows, vecs, w, w_col0=0, erows=(), evecs=(), prologue, epilogue,
              out_dtype, rc=32):
    in_specs, args = [], []
    for arr, width, cb in rows:
        in_specs.append(pl.BlockSpec((tm, width), lambda i, j, cb=cb: (i, cb)))
        args.append(arr)
    for arr, width, fn in vecs:
        in_specs.append(pl.BlockSpec((None, 1, width), lambda i, j, fn=fn: (fn(i), 0, 0)))
        args.append(arr)
    in_specs.append(pl.BlockSpec((k, tn), lambda i, j: (0, w_col0 + j)))
    args.append(w)
    for arr, cb0 in erows:
        in_specs.append(pl.BlockSpec((tm, tn), lambda i, j, cb0=cb0: (i, cb0 + j)))
        args.append(arr)
    for arr, fn in evecs:
        in_specs.append(pl.BlockSpec((None, 1, tn), lambda i, j, fn=fn: (fn(i), 0, j)))
        args.append(arr)
    scratch = [] if prologue is None else [pltpu.VMEM((tm, k), BF16)]
    kern = functools.partial(_fused_mm_kernel, n_rows=len(rows), n_vecs=len(vecs), n_erows=len(erows),
                             n_evecs=len(evecs), prologue=prologue, epilogue=epilogue, tm=tm, rc=rc)
    return pl.pallas_call(
        kern,
        out_shape=jax.ShapeDtypeStruct((m, n), out_dtype),
        grid=(m // tm, n // tn),
        in_specs=in_specs,
        out_specs=pl.BlockSpec((tm, tn), lambda i, j: (i, j)),
        scratch_shapes=scratch,
        compiler_params=_cparams(("arbitrary", "arbitrary")),
        name=name,
    )(*args)


def _rms(x, gain):
    return x * lax.rsqrt(jnp.mean(x * x, axis=-1, keepdims=True) + EPS) * gain


def _pro_norm_mod(rows, vecs):
    (x,), (gain, scale, shift) = rows, vecs
    return _rms(x, gain) * (1.0 + scale) + shift


def _pro_silu(rows, vecs):
    return _silu(rows[0])


def _epi_bias(acc, erows, evecs):
    return acc + evecs[0]


def _epi_resid(acc, erows, evecs):
    return erows[0] + evecs[0] * acc


def _kmm_kernel(*refs, n_rows, n_vecs, n_erows, prologue, epilogue, tm, rc, nk):
    rows = refs[:n_rows]
    vecs = refs[n_rows:n_rows + n_vecs]
    w_ref = refs[n_rows + n_vecs]
    p = n_rows + n_vecs + 1
    erows = refs[p:p + n_erows]
    o_ref, acc_ref, lhs_ref, ss_ref = refs[p + n_erows:p + n_erows + 4]
    k = pl.program_id(1)

    @pl.when(k == 0)
    def _():
        acc_ref[...] = jnp.zeros_like(acc_ref)
        ss_ref[...] = jnp.zeros_like(ss_ref)

    vs = [v[...] for v in vecs]

    def chunk(c, carry):
        sl = pl.ds(pl.multiple_of(c * rc, rc), rc)
        lhs, ss = prologue([r[sl, :] for r in rows], vs)
        lhs_ref[sl, :] = lhs.astype(BF16)
        if ss is not None:
            ss_ref[sl, :] += ss
        return carry

    lax.fori_loop(0, tm // rc, chunk, 0)
    acc_ref[...] += jnp.dot(lhs_ref[...], w_ref[...], preferred_element_type=F32)

    @pl.when(k == nk - 1)
    def _():
        o_ref[...] = epilogue(acc_ref[...], ss_ref[...], [e[...] for e in erows]).astype(o_ref.dtype)


def _kmm(name, m, k, n, *, tm, tk, rows, vecs, w, erows, prologue, epilogue, out_dtype, rc=64):
    in_specs, args = [], []
    for arr, cb0 in rows:
        in_specs.append(pl.BlockSpec((tm, tk), lambda i, kk, cb0=cb0: (i, cb0 + kk)))
        args.append(arr)
    for arr in vecs:
        in_specs.append(pl.BlockSpec((1, tk), lambda i, kk: (0, kk)))
        args.append(arr)
    in_specs.append(pl.BlockSpec((tk, n), lambda i, kk: (kk, 0)))
    args.append(w)
    for arr, cb in erows:
        in_specs.append(pl.BlockSpec((tm, n), lambda i, kk, cb=cb: (i, cb)))
        args.append(arr)
    kern = functools.partial(_kmm_kernel, n_rows=len(rows), n_vecs=len(vecs), n_erows=len(erows),
                             prologue=prologue, epilogue=epilogue, tm=tm, rc=rc, nk=k // tk)
    return pl.pallas_call(
        kern,
        out_shape=jax.ShapeDtypeStruct((m, n), out_dtype),
        grid=(m // tm, k // tk),
        in_specs=in_specs,
        out_specs=pl.BlockSpec((tm, n), lambda i, kk: (i, 0)),
        scratch_shapes=[pltpu.VMEM((tm, n), F32), pltpu.VMEM((tm, tk), BF16), pltpu.VMEM((tm, 1), F32)],
        compiler_params=_cparams(("arbitrary", "arbitrary")),
        name=name,
    )(*args)


def _pro_gla_out(rows, vecs):
    (of, ob, g), (gain,) = rows, vecs
    return _rms(of + ob, gain) * _silu(g), None


def _epi_gate_a(acc, ss, erows):
    return jax.nn.sigmoid(erows[0]) * acc


def _pro_ssd_out(rows, vecs):
    (yf, yb, z), (gain,) = rows, vecs
    u = (yf + yb) * _silu(z)
    return u * gain, jnp.sum(u * u, axis=-1, keepdims=True)


def _epi_gate_b(acc, ss, erows):
    rstd = lax.rsqrt(ss * (1.0 / D_INNER) + EPS)
    return erows[0] + jax.nn.sigmoid(erows[1]) * (acc * rstd)


def _gla_kernel(q_ref, k_ref, v_ref, dlr_ref, dw_ref, db_ref, s0_ref, o_ref, sf_ref, s_ref, *, lay, reverse):
    i = pl.program_id(1)
    blk = (lay.nb - 1 - i) if reverse else i
    is_ctx, _, _, pos, bps = _seq_info(lay, blk)
    first = (pos == bps - 1) if reverse else (pos == 0)
    last = (pos == 0) if reverse else (pos == bps - 1)

    @pl.when(first & is_ctx)
    def _():
        s_ref[...] = jnp.zeros_like(s_ref)

    @pl.when(first & jnp.logical_not(is_ctx))
    def _():
        s_ref[...] = s0_ref[...]

    x = jnp.dot(dlr_ref[...].astype(BF16), dw_ref[...], preferred_element_type=F32) + db_ref[...]
    log_a = (jnp.minimum(x, 0.0) - jnp.log1p(jnp.exp(-jnp.abs(x)))) * (1.0 / GLA_TAU)

    r = lax.broadcasted_iota(jnp.int32, (GLA_C, GLA_C), 0)
    c = lax.broadcasted_iota(jnp.int32, (GLA_C, GLA_C), 1)
    mask = (c >= r) if reverse else (c <= r)
    n_chunks = RB // GLA_C
    order = range(n_chunks - 1, -1, -1) if reverse else range(n_chunks)
    for ci in order:
        sl = slice(ci * GLA_C, (ci + 1) * GLA_C)
        b = _cumsum_rows(log_a[sl], GLA_C, reverse)
        q = q_ref[sl, :] * (GLA_DK ** -0.5)
        k = k_ref[sl, :]
        v = v_ref[sl, :].astype(BF16)
        qe = (q * jnp.exp(b)).astype(BF16)
        ke = (k * jnp.exp(-b)).astype(BF16)
        att = jnp.where(mask, lax.dot_general(qe, ke, NT, preferred_element_type=F32), 0.0)
        s = s_ref[...]
        o = jnp.dot(att.astype(BF16), v, preferred_element_type=F32)
        o = o + jnp.dot(qe, s.astype(BF16), preferred_element_type=F32)
        o_ref[sl, :] = o
        b_end = b[0:1, :] if reverse else b[GLA_C - 1:GLA_C, :]
        kd = (k * jnp.exp(b_end - b)).astype(BF16)
        e_col = jnp.exp(jnp.transpose(jnp.broadcast_to(b_end, (LANE, GLA_DK))))
        upd = lax.dot_general(kd, v, TN, preferred_element_type=F32)
        s_ref[...] = s * jnp.concatenate([e_col] * (GLA_DV // LANE), axis=1) + upd

    @pl.when(last & is_ctx)
    def _():
        sf_ref[...] = s_ref[...]


def _gla_scan(lay, proj, dw2p, db, s0, layer, *, reverse, name):
    d = 1 if reverse else 0
    blk = (lambda i: lay.nb - 1 - i) if reverse else (lambda i: i)
    cseq = lambda i: _seq_info(lay, blk(i))[1]
    lseq = lambda i: _seq_info(lay, blk(i))[2]
    return pl.pallas_call(
        functools.partial(_gla_kernel, lay=lay, reverse=reverse),
        out_shape=(jax.ShapeDtypeStruct((lay.m, GLA_V), F32),
                   jax.ShapeDtypeStruct((lay.n_ctx, GLA_H, GLA_DK, GLA_DV), F32)),
        grid=(GLA_H, lay.nb),
        in_specs=[
            pl.BlockSpec((RB, GLA_DK), lambda h, i: (blk(i), C_Q // GLA_DK + h)),
            pl.BlockSpec((RB, GLA_DK), lambda h, i: (blk(i), C_K // GLA_DK + h)),
            pl.BlockSpec((RB, GLA_DV), lambda h, i: (blk(i), C_V // GLA_DV + h)),
            pl.BlockSpec((RB, LANE), lambda h, i: (blk(i), C_DLR // LANE + d)),
            pl.BlockSpec((None, LANE, GLA_DK), lambda h, i: (d, 0, h)),
            pl.BlockSpec((None, 1, GLA_DK), lambda h, i: (d, 0, h)),
            pl.BlockSpec((None, None, None, None, GLA_DK, GLA_DV), lambda h, i: (lseq(i), layer, d, h, 0, 0)),
        ],
        out_specs=(pl.BlockSpec((RB, GLA_DV), lambda h, i: (blk(i), h)),
                   pl.BlockSpec((None, None, GLA_DK, GLA_DV), lambda h, i: (cseq(i), h, 0, 0))),
        scratch_shapes=[pltpu.VMEM((GLA_DK, GLA_DV), F32)],
        compiler_params=_cparams(("arbitrary", "arbitrary")),
        name=name,
    )(proj, proj, proj, proj, dw2p, db, s0)


CONV_TC = 2048
CONV_SC = 512


def _conv_kernel(prev_ref, cur_ref, next_ref, w_ref, b_ref, o_ref, *, lay):
    rb = pl.program_id(0)
    _, _, _, pos, bps = _seq_info(lay, rb)
    n = RB + 2 * SUBLANE
    for s in range(CONV_TC // CONV_SC):
        cs = slice(s * CONV_SC, (s + 1) * CONV_SC)
        prev = jnp.where(pos > 0, prev_ref[:, cs], 0.0)
        nxt = jnp.where(pos < bps - 1, next_ref[:, cs], 0.0)
        ext = jnp.concatenate([prev, cur_ref[:, cs], nxt], axis=0)
        w = w_ref[:, cs]
        acc = jnp.broadcast_to(b_ref[:, cs], (RB, CONV_SC))
        for j in range(D_CONV):
            off = j - D_CONV // 2
            sh = ext if off == 0 else pltpu.roll(ext, (-off) % n, 0)
            acc = acc + w[j:j + 1, :] * sh[SUBLANE:SUBLANE + RB, :]
        o_ref[:, cs] = _silu(acc)


def _conv(lay, proj, cw, cb):
    c0 = C_XBC // CONV_TC
    rpb = RB // SUBLANE
    return pl.pallas_call(
        functools.partial(_conv_kernel, lay=lay),
        out_shape=jax.ShapeDtypeStruct((lay.m, CONV_DIM), F32),
        grid=(lay.nb, CONV_DIM // CONV_TC),
        in_specs=[
            pl.BlockSpec((SUBLANE, CONV_TC), lambda i, j: (jnp.maximum(i * rpb - 1, 0), c0 + j)),
            pl.BlockSpec((RB, CONV_TC), lambda i, j: (i, c0 + j)),
            pl.BlockSpec((SUBLANE, CONV_TC), lambda i, j: (jnp.minimum((i + 1) * rpb, lay.m // SUBLANE - 1), c0 + j)),
            pl.BlockSpec((D_CONV, CONV_TC), lambda i, j: (0, j)),
            pl.BlockSpec((1, CONV_TC), lambda i, j: (0, j)),
        ],
        out_specs=pl.BlockSpec((RB, CONV_TC), lambda i, j: (i, j)),
        compiler_params=_cparams(("arbitrary", "arbitrary")),
        name="conv",
    )(proj, proj, proj, cw, cb)


def _dtprep_kernel(raw_ref, bias_ref, alog_ref, dt_ref, cum_ref):
    dt = _softplus(raw_ref[...] + bias_ref[...])
    dta = dt * (-jnp.exp(alog_ref[...]))
    lane = lax.broadcasted_iota(jnp.int32, (SSD_C, LANE), 1)
    cums = []
    for ci in range(RB // SSD_C):
        t = dta[ci * SSD_C:(ci + 1) * SSD_C]
        cums.append(jnp.where(lane < SSD_HEADS, _cumsum_rows(t, SSD_C, False), _cumsum_rows(t, SSD_C, True)))
    cum = jnp.concatenate(cums, axis=0)
    for d in range(2):
        for g in range(SSD_G):
            base = d * SSD_HEADS + g * SSD_HPG
            sh = (LANE - base) % LANE
            dt_ref[d, g] = dt if sh == 0 else pltpu.roll(dt, sh, 1)
            cum_ref[d, g] = cum if sh == 0 else pltpu.roll(cum, sh, 1)


def _dtprep(lay, proj, dt_bias, a_log):
    out = jax.ShapeDtypeStruct((2, SSD_G, lay.m, LANE), F32)
    ospec = pl.BlockSpec((2, SSD_G, RB, LANE), lambda i: (0, 0, i, 0))
    vspec = pl.BlockSpec((1, LANE), lambda i: (0, 0))
    return pl.pallas_call(
        _dtprep_kernel,
        out_shape=(out, out),
        grid=(lay.nb,),
        in_specs=[pl.BlockSpec((RB, LANE), lambda i: (i, C_DT // LANE)), vspec, vspec],
        out_specs=(ospec, ospec),
        compiler_params=_cparams(("arbitrary",)),
        name="dtprep",
    )(proj, dt_bias, a_log)


def _ssd_kernel(x_ref, b_ref, c_ref, dt_ref, cum_ref, h0_ref, dsk_ref, y_ref, hf_ref, h_ref, *, lay, reverse):
    i = pl.program_id(1)
    blk = (lay.nb - 1 - i) if reverse else i
    is_ctx, _, _, pos, bps = _seq_info(lay, blk)
    first = (pos == bps - 1) if reverse else (pos == 0)
    last = (pos == 0) if reverse else (pos == bps - 1)

    @pl.when(first & is_ctx)
    def _():
        h_ref[...] = jnp.zeros_like(h_ref)

    @pl.when(first & jnp.logical_not(is_ctx))
    def _():
        h_ref[...] = h0_ref[...]

    r = lax.broadcasted_iota(jnp.int32, (SSD_C, SSD_C), 0)
    c = lax.broadcasted_iota(jnp.int32, (SSD_C, SSD_C), 1)
    mask = (c >= r) if reverse else (c <= r)
    lo = c < SSD_P
    n_chunks = RB // SSD_C
    order = range(n_chunks - 1, -1, -1) if reverse else range(n_chunks)
    n_pairs = SSD_HPG // 2
    for ci in order:
        sl = slice(ci * SSD_C, (ci + 1) * SSD_C)
        x = x_ref[sl, :]
        bm = b_ref[sl, :].astype(BF16)
        cm = c_ref[sl, :].astype(BF16)
        dt = dt_ref[sl, :]
        cum = cum_ref[sl, :]
        cum_t = jnp.transpose(cum)
        cb = lax.dot_general(cm, bm, NT, preferred_element_type=F32)
        h = h_ref[...]
        y_inter = jnp.dot(cm, h.astype(BF16), preferred_element_type=F32)
        e_idx = 0 if reverse else SSD_C - 1
        ys, xdes, decs = [], [], []
        for j in range(n_pairs):
            cols, ls = [], []
            for rr in (2 * j, 2 * j + 1):
                col = jnp.broadcast_to(cum[:, rr:rr + 1], (SSD_C, SSD_C))
                seg = col - cum_t[rr:rr + 1, :]
                ls.append((jnp.exp(jnp.where(mask, seg, -1e30)) * cb).astype(BF16))
                cols.append(col)
            cp = jnp.where(lo, cols[0], cols[1])
            dtp = jnp.where(lo, jnp.broadcast_to(dt[:, 2 * j:2 * j + 1], (SSD_C, SSD_C)),
                            jnp.broadcast_to(dt[:, 2 * j + 1:2 * j + 2], (SSD_C, SSD_C)))
            xp = x[:, j * LANE:(j + 1) * LANE]
            xdt = xp * dtp
            rhs = jnp.concatenate([jnp.where(lo, xdt, 0.0), jnp.where(lo, 0.0, xdt)], axis=0).astype(BF16)
            lhs = jnp.concatenate(ls, axis=1)
            y = jnp.dot(lhs, rhs, preferred_element_type=F32)
            y = y + jnp.exp(cp) * y_inter[:, j * LANE:(j + 1) * LANE]
            if not reverse:
                y = y + dsk_ref[:, j * LANE:(j + 1) * LANE] * xp
            ys.append(y)
            edge = cp[e_idx:e_idx + 1, :]
            xdes.append((xdt * jnp.exp(edge - cp)).astype(BF16))
            decs.append(jnp.exp(edge))
        y_ref[sl, :] = jnp.concatenate(ys, axis=1)
        upd = lax.dot_general(bm, jnp.concatenate(xdes, axis=1), TN, preferred_element_type=F32)
        h_ref[...] = h * jnp.concatenate(decs, axis=1) + upd

    @pl.when(last & is_ctx)
    def _():
        hf_ref[...] = h_ref[...]


def _ssd_scan(lay, xbc, dtg, cumg, h0, dsk, *, reverse, name):
    d = 1 if reverse else 0
    blk = (lambda i: lay.nb - 1 - i) if reverse else (lambda i: i)
    cseq = lambda i: _seq_info(lay, blk(i))[1]
    lseq = lambda i: _seq_info(lay, blk(i))[2]
    gw = SSD_HPG * SSD_P
    return pl.pallas_call(
        functools.partial(_ssd_kernel, lay=lay, reverse=reverse),
        out_shape=(jax.ShapeDtypeStruct((lay.m, D_INNER), F32),
                   jax.ShapeDtypeStruct((lay.n_ctx, SSD_G, SSD_N, gw), F32)),
        grid=(SSD_G, lay.nb),
        in_specs=[
            pl.BlockSpec((RB, gw), lambda g, i: (blk(i), g)),
            pl.BlockSpec((RB, SSD_N), lambda g, i: (blk(i), D_INNER // SSD_N + g)),
            pl.BlockSpec((RB, SSD_N), lambda g, i: (blk(i), (D_INNER + SSD_BC) // SSD_N + g)),
            pl.BlockSpec((None, None, RB, LANE), lambda g, i: (d, g, blk(i), 0)),
            pl.BlockSpec((None, None, RB, LANE), lambda g, i: (d, g, blk(i), 0)),
            pl.BlockSpec((None, None, None, SSD_N, gw), lambda g, i: (lseq(i), d, g, 0, 0)),
            pl.BlockSpec((1, gw), lambda g, i: (0, g)),
        ],
        out_specs=(pl.BlockSpec((RB, gw), lambda g, i: (blk(i), g)),
                   pl.BlockSpec((None, None, SSD_N, gw), lambda g, i: (cseq(i), g, 0, 0))),
        scratch_shapes=[pltpu.VMEM((SSD_N, gw), F32)],
        compiler_params=_cparams(("arbitrary", "arbitrary")),
        name=name,
    )(xbc, xbc, xbc, dtg, cumg, h0, dsk)


def _split3(x):
    hi = x.astype(BF16)
    r1 = x - hi.astype(F32)
    mid = r1.astype(BF16)
    lo = (r1 - mid.astype(F32)).astype(BF16)
    return hi, mid, lo


def _norm_router_kernel(x_ref, gain_ref, scale_ref, shift_ref, wr_ref, h_ref, lg_ref):
    h = _rms(x_ref[...], gain_ref[...]) * (1.0 + scale_ref[...]) + shift_ref[...]
    h_ref[...] = h.astype(BF16)
    hs = _split3(h)
    ws = _split3(wr_ref[...])
    acc = jnp.zeros(lg_ref.shape, F32)
    for a, b in ((2, 0), (0, 2), (1, 1), (1, 0), (0, 1), (0, 0)):
        acc = acc + jnp.dot(hs[a], ws[b], preferred_element_type=F32)
    lg_ref[...] = acc


def _norm_router(lay, x, gain, mod3, wr):
    mrow = lambda i: _mod_row(lay, i * RB)
    vspec = lambda fn: pl.BlockSpec((None, 1, D), lambda i: (fn(i), 0, 0))
    return pl.pallas_call(
        _norm_router_kernel,
        out_shape=(jax.ShapeDtypeStruct((lay.m, D), BF16), jax.ShapeDtypeStruct((lay.m, LANE), F32)),
        grid=(lay.nb,),
        in_specs=[pl.BlockSpec((RB, D), lambda i: (i, 0)), vspec(lambda i: 0),
                  vspec(lambda i: mrow(i) * 6 + 4), vspec(lambda i: mrow(i) * 6 + 3),
                  pl.BlockSpec((D, LANE), lambda i: (0, 0))],
        out_specs=(pl.BlockSpec((RB, D), lambda i: (i, 0)), pl.BlockSpec((RB, LANE), lambda i: (i, 0))),
        compiler_params=_cparams(("arbitrary",)),
        name="norm_router",
    )(x, gain, mod3, mod3, wr)


def _expert_kernel(be_ref, nu_ref, x_ref, w1_ref, w3_ref, w2_ref, o_ref):
    b = pl.program_id(0)

    @pl.when(b < nu_ref[0])
    def _():
        x = x_ref[...]
        h1 = jnp.dot(x, w1_ref[...], preferred_element_type=F32)
        h3 = jnp.dot(x, w3_ref[...], preferred_element_type=F32)
        hid = (_silu(h1) * h3).astype(BF16)
        o_ref[...] = jnp.dot(hid, w2_ref[...], preferred_element_type=F32)

    @pl.when(b >= nu_ref[0])
    def _():
        o_ref[...] = jnp.zeros_like(o_ref)


def _experts(xb, w1, w3, w2, block_e, n_used):
    n_rows = xb.shape[0]
    n_blocks = n_rows // MOE_BLOCK
    wspec = lambda a, b: pl.BlockSpec((None, a, b), lambda i, be, nu: (be[i], 0, 0))
    return pl.pallas_call(
        _expert_kernel,
        out_shape=jax.ShapeDtypeStruct((n_rows, D), F32),
        grid_spec=pltpu.PrefetchScalarGridSpec(
            num_scalar_prefetch=2,
            grid=(n_blocks,),
            in_specs=[pl.BlockSpec((MOE_BLOCK, D), lambda i, be, nu: (i, 0)),
                      wspec(D, D_EXP), wspec(D, D_EXP), wspec(D_EXP, D)],
            out_specs=pl.BlockSpec((MOE_BLOCK, D), lambda i, be, nu: (i, 0)),
        ),
        compiler_params=_cparams(("arbitrary",)),
        name="experts",
    )(block_e, n_used, xb, w1, w3, w2)


def _moe_residual(x_ref, y0_ref, y1_ref, tw_ref, gate_ref):
    tw = tw_ref[...]
    return x_ref[...] + gate_ref[...] * (y0_ref[...] * tw[:, 0:1] + y1_ref[...] * tw[:, 1:2])


def _combine_kernel(x_ref, y0_ref, y1_ref, tw_ref, gate_ref, o_ref):
    o_ref[...] = _moe_residual(x_ref, y0_ref, y1_ref, tw_ref, gate_ref)


def _combine(lay, x, y0, y1, tw, mod3):
    mrow = lambda i: _mod_row(lay, i * RB)
    spec = pl.BlockSpec((RB, D), lambda i: (i, 0))
    return pl.pallas_call(
        _combine_kernel,
        out_shape=jax.ShapeDtypeStruct((lay.m, D), F32),
        grid=(lay.nb,),
        in_specs=[spec, spec, spec, pl.BlockSpec((RB, TOP_K), lambda i: (i, 0)),
                  pl.BlockSpec((None, 1, D), lambda i: (mrow(i) * 6 + 5, 0, 0))],
        out_specs=spec,
        compiler_params=_cparams(("arbitrary",)),
        name="combine",
    )(x, y0, y1, tw, mod3)


def _final_kernel(x_ref, y0_ref, y1_ref, tw_ref, gate_ref, nf_ref, op_ref, os_ref, *, lay):
    rb = pl.program_id(0)
    y = _rms(_moe_residual(x_ref, y0_ref, y1_ref, tw_ref, gate_ref), nf_ref[...])

    @pl.when(rb < lay.nb_ctx)
    def _():
        op_ref[...] = y

    @pl.when(rb >= lay.nb_ctx)
    def _():
        os_ref[...] = y


def _final(lay, x, y0, y1, tw, mod3, norm_f):
    mrow = lambda i: _mod_row(lay, i * RB)
    spec = pl.BlockSpec((RB, D), lambda i: (i, 0))
    return pl.pallas_call(
        functools.partial(_final_kernel, lay=lay),
        out_shape=(jax.ShapeDtypeStruct((lay.m_ctx, D), F32), jax.ShapeDtypeStruct((lay.m - lay.m_ctx, D), F32)),
        grid=(lay.nb,),
        in_specs=[spec, spec, spec, pl.BlockSpec((RB, TOP_K), lambda i: (i, 0)),
                  pl.BlockSpec((None, 1, D), lambda i: (mrow(i) * 6 + 5, 0, 0)),
                  pl.BlockSpec((1, D), lambda i: (0, 0))],
        out_specs=(pl.BlockSpec((RB, D), lambda i: (jnp.minimum(i, lay.nb_ctx - 1), 0)),
                   pl.BlockSpec((RB, D), lambda i: (jnp.maximum(i - lay.nb_ctx, 0), 0))),
        compiler_params=_cparams(("arbitrary",)),
        name="final",
    )(x, y0, y1, tw, mod3, norm_f)


def _route(logits, router_bias, n_tok):
    n_asg = n_tok * TOP_K
    scores = jax.nn.sigmoid(logits)
    biased = scores + router_bias.astype(F32)
    bg = biased.reshape(n_tok, N_GRP, EPG)
    pair_sums = [bg[..., a] + bg[..., b] for a in range(EPG) for b in range(a + 1, EPG)]
    group_score = functools.reduce(jnp.maximum, pair_sums)
    best_group = jnp.argmax(group_score, axis=-1)
    sel = (jnp.arange(N_GRP)[None, :, None] == best_group[:, None, None])
    vb = jnp.sum(jnp.where(sel, bg, 0.0), axis=1)
    vs = jnp.sum(jnp.where(sel, scores.reshape(n_tok, N_GRP, EPG), 0.0), axis=1)
    lane = jnp.arange(EPG)[None, :]
    i1 = jnp.argmax(vb, axis=-1)
    i2 = jnp.argmax(jnp.where(lane == i1[:, None], -jnp.inf, vb), axis=-1)
    top_loc = jnp.stack([i1, i2], axis=-1)
    top_idx = (best_group[:, None] * EPG + top_loc).astype(jnp.int32)
    top_w = jnp.sum(jnp.where(lane[:, None, :] == top_loc[:, :, None], vs[:, None, :], 0.0), axis=-1)
    top_w = top_w / jnp.sum(top_w, axis=-1, keepdims=True)
    e_flat = top_idx.reshape(-1)
    tok_flat = jnp.arange(n_asg, dtype=jnp.int32) // TOP_K
    cb = MOE_BLOCK
    oh = (e_flat[:, None] == jnp.arange(N_EXP, dtype=e_flat.dtype)[None, :]).astype(F32).reshape(n_asg // cb, cb, N_EXP)
    tri = (jnp.arange(cb)[:, None] > jnp.arange(cb)[None, :]).astype(F32)
    within = jnp.einsum('ts,bse->bte', tri, oh)
    bsum = oh.sum(1)
    counts = bsum.sum(0).astype(jnp.int32)
    padded = (counts + MOE_BLOCK - 1) // MOE_BLOCK * MOE_BLOCK
    padded_end = jnp.cumsum(padded)
    padded_start = padded_end - padded
    base = jnp.cumsum(bsum, axis=0) - bsum + padded_start.astype(F32)[None, :]
    dest = ((within + base[:, None, :]) * oh).sum(-1).reshape(n_asg).astype(jnp.int32)
    n_blocks = -(-n_asg // MOE_BLOCK) + N_EXP
    n_rows = n_blocks * MOE_BLOCK
    row_tok = jnp.zeros((n_rows,), jnp.int32).at[dest].set(tok_flat)
    block_start = jnp.arange(n_blocks, dtype=jnp.int32) * MOE_BLOCK
    block_e = jnp.minimum(jnp.searchsorted(padded_end, block_start, side='right'), N_EXP - 1).astype(jnp.int32)
    n_used = (padded_end[-1] // MOE_BLOCK).astype(jnp.int32).reshape(1)
    return row_tok, top_w, block_e, n_used, dest.reshape(n_tok, TOP_K)


def _sincos_2d(rows, cols, dim):
    quarter = dim // 4
    omega = 1.0 / (POS_BASE ** (jnp.arange(quarter, dtype=F32) / quarter))

    def axis_embed(n):
        ang = jnp.arange(n, dtype=F32)[:, None] * omega[None, :]
        return jnp.concatenate([jnp.sin(ang), jnp.cos(ang)], axis=-1)

    er = jnp.broadcast_to(axis_embed(rows)[:, None, :], (rows, cols, dim // 2))
    ec = jnp.broadcast_to(axis_embed(cols)[None, :, :], (rows, cols, dim // 2))
    return jnp.concatenate([er, ec], axis=-1).reshape(rows * cols, dim)


def _rearranged_w_in(w):
    sizes = (GLA_QK, GLA_QK, GLA_V, GLA_V, 2 * GLA_LR, D_INNER, CONV_DIM, 2 * SSD_HEADS, D, D)
    offs = [0]
    for s in sizes:
        offs.append(offs[-1] + s)
    q, k, v, g, dlr, z, xbc, dt, ga, gb = [w[:, offs[i]:offs[i + 1]] for i in range(len(sizes))]
    zpad = jnp.zeros((w.shape[0], LANE - GLA_LR), w.dtype)
    return jnp.concatenate([q, k, v, g, ga, z, xbc, gb, dt, dlr[:, :GLA_LR], zpad, dlr[:, GLA_LR:], zpad,
                            jnp.zeros((w.shape[0], LANE), w.dtype)], axis=1).astype(BF16)


def _pick_tile(lay, want):
    t = want
    while lay.m_ctx % t or lay.t_lat % t:
        t //= 2
    return t


def kernel(x_prompt, x_sample, c, state_gla, state_ssd, c_ctx, w_ada, b_ada, norm1, norm2, w_in, gla_dw2, gla_db, gla_norm, w_gla_b, conv_w, conv_b, a_log, dt_bias, d_skip, ssd_norm, w_ssd_b, w_out, w_router, router_bias, w_e1, w_e3, w_e2, norm_f):
    n_ctx, t_ctx, _ = x_prompt.shape
    n_lat, t_lat, _ = x_sample.shape
    lay = Layout(n_ctx, t_ctx, n_lat, t_lat)
    depth = w_in.shape[0]
    m = lay.m
    assert t_ctx % RB == 0 and t_lat % RB == 0 and t_lat % GRID_W == 0

    pos = _sincos_2d(t_lat // GRID_W, GRID_W, D)
    x = _embed(lay, x_prompt.reshape(lay.m_ctx, D), x_sample.reshape(n_lat * t_lat, D), pos)

    n_mod = 16
    cond = jnp.concatenate([c_ctx[None, :], c, jnp.zeros((n_mod - 1 - n_lat, D), F32)], axis=0)
    wr = jnp.pad(w_router, ((0, 0), (0, LANE - N_EXP)))

    tm_in = _pick_tile(lay, 1024)
    tm_a = _pick_tile(lay, 512)
    mrow = lambda tm: (lambda i: _mod_row(lay, i * tm))
    new_gla, new_ssd = [], []
    y_prompt = y_sample = None
    for l in range(depth):
        mod = _fused_mm("ada", n_mod, D, 6 * D, tm=n_mod, tn=1024, rows=[(cond, D, 0)], vecs=[],
                        w=w_ada[l].astype(BF16), evecs=[(b_ada[l].reshape(1, 1, 6 * D), lambda i: 0)],
                        prologue=_pro_silu, epilogue=_epi_bias, out_dtype=F32, rc=n_mod)
        mod3 = mod.reshape(n_mod * 6, 1, D)
        n1 = norm1[l].reshape(1, 1, D)
        mr = mrow(tm_in)
        proj = _fused_mm("in_proj", m, D, NP, tm=tm_in, tn=512, rows=[(x, D, 0)],
                         vecs=[(n1, D, lambda i: 0), (mod3, D, lambda i: mr(i) * 6 + 1),
                               (mod3, D, lambda i: mr(i) * 6 + 0)],
                         w=_rearranged_w_in(w_in[l]), prologue=_pro_norm_mod, epilogue=None, out_dtype=F32)

        dw2p = jnp.pad(gla_dw2[l], ((0, 0), (0, LANE - GLA_LR), (0, 0))).astype(BF16)
        db = gla_db[l].reshape(2, 1, GLA_QK)
        o_f, sg_f = _gla_scan(lay, proj, dw2p, db, state_gla, l, reverse=False, name="gla_fwd")
        o_b, sg_b = _gla_scan(lay, proj, dw2p, db, state_gla, l, reverse=True, name="gla_bwd")
        new_gla.append(jnp.stack([sg_f, sg_b], axis=1))
        m_a = _kmm("gla_out", m, GLA_V, D, tm=tm_a, tk=GLA_DV,
                   rows=[(o_f, 0), (o_b, 0), (proj, C_G // GLA_DV)], vecs=[gla_norm[l].reshape(1, GLA_V)],
                   w=w_gla_b[l].astype(BF16), erows=[(proj, C_GA // D)],
                   prologue=_pro_gla_out, epilogue=_epi_gate_a, out_dtype=F32)

        xbc = _conv(lay, proj, conv_w[l], conv_b[l].reshape(1, CONV_DIM))
        dtg, cumg = _dtprep(lay, proj, dt_bias[l].reshape(1, LANE), a_log[l].reshape(1, LANE))
        h0 = state_ssd[:, l].reshape(n_lat, 2, SSD_G, SSD_HPG, SSD_N, SSD_P)
        h0 = h0.transpose(0, 1, 2, 4, 3, 5).reshape(n_lat, 2, SSD_G, SSD_N, SSD_HPG * SSD_P)
        dsk = jnp.repeat(d_skip[l], SSD_P).reshape(1, D_INNER)
        y_f, hs_f = _ssd_scan(lay, xbc, dtg, cumg, h0, dsk, reverse=False, name="ssd_fwd")
        y_b, hs_b = _ssd_scan(lay, xbc, dtg, cumg, h0, dsk, reverse=True, name="ssd_bwd")
        hs = jnp.stack([hs_f, hs_b], axis=1).reshape(n_ctx, 2, SSD_G, SSD_N, SSD_HPG, SSD_P)
        new_ssd.append(hs.transpose(0, 1, 2, 4, 3, 5).reshape(n_ctx, 2, SSD_HEADS, SSD_N, SSD_P))
        merged = _kmm("ssd_out", m, D_INNER, D, tm=tm_a, tk=512,
                      rows=[(y_f, 0), (y_b, 0), (proj, C_Z // 512)], vecs=[ssd_norm[l].reshape(1, D_INNER)],
                      w=w_ssd_b[l].astype(BF16), erows=[(m_a, 0), (proj, C_GB // D)],
                      prologue=_pro_ssd_out, epilogue=_epi_gate_b, out_dtype=BF16)

        x = _fused_mm("mix_out", m, D, D, tm=tm_in, tn=512, rows=[(merged, D, 0)], vecs=[],
                      w=w_out[l].astype(BF16), erows=[(x, 0)], evecs=[(mod3, lambda i: mr(i) * 6 + 2)],
                      prologue=None, epilogue=_epi_resid, out_dtype=F32)

        h2, logits = _norm_router(lay, x, norm2[l].reshape(1, 1, D), mod3, wr)
        row_tok, top_w, block_e, n_used, pos_tk = _route(logits[:, :N_EXP], router_bias, m)
        xb = h2[row_tok]
        yb = _experts(xb, w_e1[l].astype(BF16), w_e3[l].astype(BF16), w_e2[l].astype(BF16), block_e, n_used)
        y0 = yb[pos_tk[:, 0]]
        y1 = yb[pos_tk[:, 1]]
        if l + 1 < depth:
            x = _combine(lay, x, y0, y1, top_w, mod3)
        else:
            y_prompt, y_sample = _final(lay, x, y0, y1, top_w, mod3, norm_f.reshape(1, D))

    return (y_prompt.reshape(n_ctx, t_ctx, D), y_sample.reshape(n_lat, t_lat, D),
            jnp.stack(new_gla, axis=1), jnp.stack(new_ssd, axis=1))
```

```python
import functools
import math
from typing import NamedTuple

import jax
import jax.numpy as jnp
from jax import lax
from jax.experimental import pallas as pl
from jax.experimental.pallas import tpu as pltpu

F32 = jnp.float32
BF16 = jnp.bfloat16

D = 2048
GRID_W = 64
POS_BASE = 10000.0
EPS = 1e-6
GLA_H = 4
GLA_DK = D // 2 // GLA_H
GLA_DV = D // GLA_H
GLA_QK = GLA_H * GLA_DK
GLA_V = GLA_H * GLA_DV
GLA_LR = 16
GLA_TAU = 16.0
GLA_C = 64
D_INNER = 2 * D
SSD_P = 64
SSD_HEADS = D_INNER // SSD_P
SSD_N = 128
SSD_G = 8
SSD_HPG = SSD_HEADS // SSD_G
SSD_BC = SSD_G * SSD_N
CONV_DIM = D_INNER + 2 * SSD_BC
D_CONV = 5
SSD_C = 128
N_EXP = 16
N_GRP = 4
EPG = N_EXP // N_GRP
TOP_K = 2
D_EXP = D // 2
MOE_BLOCK = 256

LANE = 128
SUBLANE = 8
RB = 256
VMEM_LIMIT = 56 * 1024 * 1024

C_Q = 0
C_K = C_Q + GLA_QK
C_V = C_K + GLA_QK
C_G = C_V + GLA_V
C_GA = C_G + GLA_V
C_Z = C_GA + D
C_XBC = C_Z + D_INNER
C_B = C_XBC + D_INNER
C_CM = C_B + SSD_BC
C_GB = C_XBC + CONV_DIM
C_DT = C_GB + D
C_DLR = C_DT + 2 * SSD_HEADS
NP = C_DLR + 3 * LANE


class Layout(NamedTuple):
    n_ctx: int
    t_ctx: int
    n_lat: int
    t_lat: int

    @property
    def m_ctx(self):
        return self.n_ctx * self.t_ctx

    @property
    def m(self):
        return self.m_ctx + self.n_lat * self.t_lat

    @property
    def nb(self):
        return self.m // RB

    @property
    def nb_ctx(self):
        return self.m_ctx // RB

    @property
    def bps_ctx(self):
        return self.t_ctx // RB

    @property
    def bps_lat(self):
        return self.t_lat // RB


def _seq_info(lay, blk):
    is_ctx = blk < lay.nb_ctx
    lb = jnp.maximum(blk - lay.nb_ctx, 0)
    cb = jnp.minimum(blk, lay.nb_ctx - 1)
    pos = jnp.where(is_ctx, cb % lay.bps_ctx, lb % lay.bps_lat)
    bps = jnp.where(is_ctx, lay.bps_ctx, lay.bps_lat)
    return is_ctx, cb // lay.bps_ctx, jnp.minimum(lb // lay.bps_lat, lay.n_lat - 1), pos, bps


def _mod_row(lay, row_start):
    return jnp.where(row_start < lay.m_ctx, 0, 1 + jnp.maximum(row_start - lay.m_ctx, 0) // lay.t_lat)


def _cparams(sem):
    return pltpu.CompilerParams(dimension_semantics=sem, vmem_limit_bytes=VMEM_LIMIT)


def _silu(x):
    return x * jax.nn.sigmoid(x)


def _softplus(x):
    return jnp.maximum(x, 0.0) + jnp.log1p(jnp.exp(-jnp.abs(x)))


def _cumsum_rows(x, n, reverse):
    rows = lax.broadcasted_iota(jnp.int32, x.shape, 0)
    s = 1
    while s < n:
        if reverse:
            x = x + jnp.where(rows < n - s, pltpu.roll(x, n - s, 0), 0.0)
        else:
            x = x + jnp.where(rows >= s, pltpu.roll(x, s, 0), 0.0)
        s *= 2
    return x


NT = (((1,), (1,)), ((), ()))
TN = (((0,), (0,)), ((), ()))


def _embed_kernel(xp_ref, xs_ref, pos_ref, o_ref, *, lay):
    rb = pl.program_id(0)

    @pl.when(rb < lay.nb_ctx)
    def _():
        o_ref[...] = xp_ref[...]

    @pl.when(rb >= lay.nb_ctx)
    def _():
        o_ref[...] = xs_ref[...] + pos_ref[...]


def _embed(lay, xp, xs, pos):
    spec = lambda f: pl.BlockSpec((RB, D), f)
    return pl.pallas_call(
        functools.partial(_embed_kernel, lay=lay),
        out_shape=jax.ShapeDtypeStruct((lay.m, D), F32),
        grid=(lay.nb,),
        in_specs=[spec(lambda i: (jnp.minimum(i, lay.nb_ctx - 1), 0)),
                  spec(lambda i: (jnp.maximum(i - lay.nb_ctx, 0), 0)),
                  spec(lambda i: (jnp.maximum(i - lay.nb_ctx, 0) % lay.bps_lat, 0))],
        out_specs=spec(lambda i: (i, 0)),
        compiler_params=_cparams(("arbitrary",)),
        name="embed",
    )(xp, xs, pos)


def _fused_mm_kernel(*refs, n_rows, n_vecs, n_erows, n_evecs, prologue, epilogue, tm, rc):
    rows = refs[:n_rows]
    vecs = refs[n_rows:n_rows + n_vecs]
    w_ref = refs[n_rows + n_vecs]
    p = n_rows + n_vecs + 1
    erows = refs[p:p + n_erows]
    evecs = refs[p + n_erows:p + n_erows + n_evecs]
    o_ref = refs[p + n_erows + n_evecs]

    if prologue is None:
        lhs = rows[0][...]
    else:
        h_ref = refs[p + n_erows + n_evecs + 1]

        @pl.when(pl.program_id(1) == 0)
        def _():
            vs = [v[...] for v in vecs]

            def chunk(c, carry):
                sl = pl.ds(pl.multiple_of(c * rc, rc), rc)
                h_ref[sl, :] = prologue([r[sl, :] for r in rows], vs).astype(BF16)
                return carry

            lax.fori_loop(0, tm // rc, chunk, 0)

        lhs = h_ref[...]
    acc = jnp.dot(lhs, w_ref[...], preferred_element_type=F32)
    if epilogue is not None:
        acc = epilogue(acc, [e[...] for e in erows], [v[...] for v in evecs])
    o_ref[...] = acc.astype(o_ref.dtype)


def _fused_mm(name, m, k, n, *, tm, tn, rows, vecs, w, w_col0=0, erows=(), evecs=(), prologue, epilogue,
              out_dtype, rc=32):
    in_specs, args = [], []
    for arr, width, cb in rows:
        in_specs.append(pl.BlockSpec((tm, width), lambda i, j, cb=cb: (i, cb)))
        args.append(arr)
    for arr, width, fn in vecs:
        in_specs.append(pl.BlockSpec((None, 1, width), lambda i, j, fn=fn: (fn(i), 0, 0)))
        args.append(arr)
    in_specs.append(pl.BlockSpec((k, tn), lambda i, j: (0, w_col0 + j)))
    args.append(w)
    for arr, cb0 in erows:
        in_specs.append(pl.BlockSpec((tm, tn), lambda i, j, cb0=cb0: (i, cb0 + j)))
        args.append(arr)
    for arr, fn in evecs:
        in_specs.append(pl.BlockSpec((None, 1, tn), lambda i, j, fn=fn: (fn(i), 0, j)))
        args.append(arr)
    scratch = [] if prologue is None else [pltpu.VMEM((tm, k), BF16)]
    kern = functools.partial(_fused_mm_kernel, n_rows=len(rows), n_vecs=len(vecs), n_erows=len(erows),
                             n_evecs=len(evecs), prologue=prologue, epilogue=epilogue, tm=tm, rc=rc)
    return pl.pallas_call(
        kern,
        out_shape=jax.ShapeDtypeStruct((m, n), out_dtype),
        grid=(m // tm, n // tn),
        in_specs=in_specs,
        out_specs=pl.BlockSpec((tm, tn), lambda i, j: (i, j)),
        scratch_shapes=scratch,
        compiler_params=_cparams(("arbitrary", "arbitrary")),
        name=name,
    )(*args)


def _rms(x, gain):
    return x * lax.rsqrt(jnp.mean(x * x, axis=-1, keepdims=True) + EPS) * gain


def _pro_norm_mod(rows, vecs):
    (x,), (gain, scale, shift) = rows, vecs
    return _rms(x, gain) * (1.0 + scale) + shift


def _pro_silu(rows, vecs):
    return _silu(rows[0])


def _epi_bias(acc, erows, evecs):
    return acc + evecs[0]


def _epi_resid(acc, erows, evecs):
    return erows[0] + evecs[0] * acc


def _kmm_kernel(*refs, n_rows, n_vecs, n_erows, prologue, epilogue, tm, rc, nk):
    rows = refs[:n_rows]
    vecs = refs[n_rows:n_rows + n_vecs]
    w_ref = refs[n_rows + n_vecs]
    p = n_rows + n_vecs + 1
    erows = refs[p:p + n_erows]
    o_ref, acc_ref, lhs_ref, ss_ref = refs[p + n_erows:p + n_erows + 4]
    k = pl.program_id(1)

    @pl.when(k == 0)
    def _():
        acc_ref[...] = jnp.zeros_like(acc_ref)
        ss_ref[...] = jnp.zeros_like(ss_ref)

    vs = [v[...] for v in vecs]

    def chunk(c, carry):
        sl = pl.ds(pl.multiple_of(c * rc, rc), rc)
        lhs, ss = prologue([r[sl, :] for r in rows], vs)
        lhs_ref[sl, :] = lhs.astype(BF16)
        if ss is not None:
            ss_ref[sl, :] += ss
        return carry

    lax.fori_loop(0, tm // rc, chunk, 0)
    acc_ref[...] += jnp.dot(lhs_ref[...], w_ref[...], preferred_element_type=F32)

    @pl.when(k == nk - 1)
    def _():
        o_ref[...] = epilogue(acc_ref[...], ss_ref[...], [e[...] for e in erows]).astype(o_ref.dtype)


def _kmm(name, m, k, n, *, tm, tk, rows, vecs, w, erows, prologue, epilogue, out_dtype, rc=64):
    in_specs, args = [], []
    for arr, cb0 in rows:
        in_specs.append(pl.BlockSpec((tm, tk), lambda i, kk, cb0=cb0: (i, cb0 + kk)))
        args.append(arr)
    for arr in vecs:
        in_specs.append(pl.BlockSpec((1, tk), lambda i, kk: (0, kk)))
        args.append(arr)
    in_specs.append(pl.BlockSpec((tk, n), lambda i, kk: (kk, 0)))
    args.append(w)
    for arr, cb in erows:
        in_specs.append(pl.BlockSpec((tm, n), lambda i, kk, cb=cb: (i, cb)))
        args.append(arr)
    kern = functools.partial(_kmm_kernel, n_rows=len(rows), n_vecs=len(vecs), n_erows=len(erows),
                             prologue=prologue, epilogue=epilogue, tm=tm, rc=rc, nk=k // tk)
    return pl.pallas_call(
        kern,
        out_shape=jax.ShapeDtypeStruct((m, n), out_dtype),
        grid=(m // tm, k // tk),
        in_specs=in_specs,
        out_specs=pl.BlockSpec((tm, n), lambda i, kk: (i, 0)),
        scratch_shapes=[pltpu.VMEM((tm, n), F32), pltpu.VMEM((tm, tk), BF16), pltpu.VMEM((tm, 1), F32)],
        compiler_params=_cparams(("arbitrary", "arbitrary")),
        name=name,
    )(*args)


def _pro_gla_out(rows, vecs):
    (of, ob, g), (gain,) = rows, vecs
    return _rms(of + ob, gain) * _silu(g), None


def _epi_gate_a(acc, ss, erows):
    return jax.nn.sigmoid(erows[0]) * acc


def _pro_ssd_out(rows, vecs):
    (yf, yb, z), (gain,) = rows, vecs
    u = (yf + yb) * _silu(z)
    return u * gain, jnp.sum(u * u, axis=-1, keepdims=True)


def _epi_gate_b(acc, ss, erows):
    rstd = lax.rsqrt(ss * (1.0 / D_INNER) + EPS)
    return erows[0] + jax.nn.sigmoid(erows[1]) * (acc * rstd)


def _gla_kernel(q_ref, k_ref, v_ref, dlr_ref, dw_ref, db_ref, s0_ref, o_ref, sf_ref, s_ref, *, lay, reverse):
    i = pl.program_id(1)
    blk = (lay.nb - 1 - i) if reverse else i
    is_ctx, _, _, pos, bps = _seq_info(lay, blk)
    first = (pos == bps - 1) if reverse else (pos == 0)
    last = (pos == 0) if reverse else (pos == bps - 1)

    @pl.when(first & is_ctx)
    def _():
        s_ref[...] = jnp.zeros_like(s_ref)

    @pl.when(first & jnp.logical_not(is_ctx))
    def _():
        s_ref[...] = s0_ref[...]

    x = jnp.dot(dlr_ref[...].astype(BF16), dw_ref[...], preferred_element_type=F32) + db_ref[...]
    log_a = (jnp.minimum(x, 0.0) - jnp.log1p(jnp.exp(-jnp.abs(x)))) * (1.0 / GLA_TAU)

    r = lax.broadcasted_iota(jnp.int32, (GLA_C, GLA_C), 0)
    c = lax.broadcasted_iota(jnp.int32, (GLA_C, GLA_C), 1)
    mask = (c >= r) if reverse else (c <= r)
    n_chunks = RB // GLA_C
    order = range(n_chunks - 1, -1, -1) if reverse else range(n_chunks)
    for ci in order:
        sl = slice(ci * GLA_C, (ci + 1) * GLA_C)
        b = _cumsum_rows(log_a[sl], GLA_C, reverse)
        q = q_ref[sl, :] * (GLA_DK ** -0.5)
        k = k_ref[sl, :]
        v = v_ref[sl, :].astype(BF16)
        qe = (q * jnp.exp(b)).astype(BF16)
        ke = (k * jnp.exp(-b)).astype(BF16)
        att = jnp.where(mask, lax.dot_general(qe, ke, NT, preferred_element_type=F32), 0.0)
        s = s_ref[...]
        o = jnp.dot(att.astype(BF16), v, preferred_element_type=F32)
        o = o + jnp.dot(qe, s.astype(BF16), preferred_element_type=F32)
        o_ref[sl, :] = o
        b_end = b[0:1, :] if reverse else b[GLA_C - 1:GLA_C, :]
        kd = (k * jnp.exp(b_end - b)).astype(BF16)
        e_col = jnp.exp(jnp.transpose(jnp.broadcast_to(b_end, (LANE, GLA_DK))))
        upd = lax.dot_general(kd, v, TN, preferred_element_type=F32)
        s_ref[...] = s * jnp.concatenate([e_col] * (GLA_DV // LANE), axis=1) + upd

    @pl.when(last & is_ctx)
    def _():
        sf_ref[...] = s_ref[...]


def _gla_scan(lay, proj, dw2p, db, s0, layer, *, reverse, name):
    d = 1 if reverse else 0
    blk = (lambda i: lay.nb - 1 - i) if reverse else (lambda i: i)
    cseq = lambda i: _seq_info(lay, blk(i))[1]
    lseq = lambda i: _seq_info(lay, blk(i))[2]
    return pl.pallas_call(
        functools.partial(_gla_kernel, lay=lay, reverse=reverse),
        out_shape=(jax.ShapeDtypeStruct((lay.m, GLA_V), F32),
                   jax.ShapeDtypeStruct((lay.n_ctx, GLA_H, GLA_DK, GLA_DV), F32)),
        grid=(GLA_H, lay.nb),
        in_specs=[
            pl.BlockSpec((RB, GLA_DK), lambda h, i: (blk(i), C_Q // GLA_DK + h)),
            pl.BlockSpec((RB, GLA_DK), lambda h, i: (blk(i), C_K // GLA_DK + h)),
            pl.BlockSpec((RB, GLA_DV), lambda h, i: (blk(i), C_V // GLA_DV + h)),
            pl.BlockSpec((RB, LANE), lambda h, i: (blk(i), C_DLR // LANE + d)),
            pl.BlockSpec((None, LANE, GLA_DK), lambda h, i: (d, 0, h)),
            pl.BlockSpec((None, 1, GLA_DK), lambda h, i: (d, 0, h)),
            pl.BlockSpec((None, None, None, None, GLA_DK, GLA_DV), lambda h, i: (lseq(i), layer, d, h, 0, 0)),
        ],
        out_specs=(pl.BlockSpec((RB, GLA_DV), lambda h, i: (blk(i), h)),
                   pl.BlockSpec((None, None, GLA_DK, GLA_DV), lambda h, i: (cseq(i), h, 0, 0))),
        scratch_shapes=[pltpu.VMEM((GLA_DK, GLA_DV), F32)],
        compiler_params=_cparams(("arbitrary", "arbitrary")),
        name=name,
    )(proj, proj, proj, proj, dw2p, db, s0)


CONV_TC = 2048
CONV_SC = 512


def _conv_kernel(prev_ref, cur_ref, next_ref, w_ref, b_ref, o_ref, *, lay):
    rb = pl.program_id(0)
    _, _, _, pos, bps = _seq_info(lay, rb)
    n = RB + 2 * SUBLANE
    for s in range(CONV_TC // CONV_SC):
        cs = slice(s * CONV_SC, (s + 1) * CONV_SC)
        prev = jnp.where(pos > 0, prev_ref[:, cs], 0.0)
        nxt = jnp.where(pos < bps - 1, next_ref[:, cs], 0.0)
        ext = jnp.concatenate([prev, cur_ref[:, cs], nxt], axis=0)
        w = w_ref[:, cs]
        acc = jnp.broadcast_to(b_ref[:, cs], (RB, CONV_SC))
        for j in range(D_CONV):
            off = j - D_CONV // 2
            sh = ext if off == 0 else pltpu.roll(ext, (-off) % n, 0)
            acc = acc + w[j:j + 1, :] * sh[SUBLANE:SUBLANE + RB, :]
        o_ref[:, cs] = _silu(acc)


def _conv(lay, proj, cw, cb):
    c0 = C_XBC // CONV_TC
    rpb = RB // SUBLANE
    return pl.pallas_call(
        functools.partial(_conv_kernel, lay=lay),
        out_shape=jax.ShapeDtypeStruct((lay.m, CONV_DIM), F32),
        grid=(lay.nb, CONV_DIM // CONV_TC),
        in_specs=[
            pl.BlockSpec((SUBLANE, CONV_TC), lambda i, j: (jnp.maximum(i * rpb - 1, 0), c0 + j)),
            pl.BlockSpec((RB, CONV_TC), lambda i, j: (i, c0 + j)),
            pl.BlockSpec((SUBLANE, CONV_TC), lambda i, j: (jnp.minimum((i + 1) * rpb, lay.m // SUBLANE - 1), c0 + j)),
            pl.BlockSpec((D_CONV, CONV_TC), lambda i, j: (0, j)),
            pl.BlockSpec((1, CONV_TC), lambda i, j: (0, j)),
        ],
        out_specs=pl.BlockSpec((RB, CONV_TC), lambda i, j: (i, j)),
        compiler_params=_cparams(("arbitrary", "arbitrary")),
        name="conv",
    )(proj, proj, proj, cw, cb)


def _dtprep_kernel(raw_ref, bias_ref, alog_ref, dt_ref, cum_ref):
    dt = _softplus(raw_ref[...] + bias_ref[...])
    dta = dt * (-jnp.exp(alog_ref[...]))
    lane = lax.broadcasted_iota(jnp.int32, (SSD_C, LANE), 1)
    cums = []
    for ci in range(RB // SSD_C):
        t = dta[ci * SSD_C:(ci + 1) * SSD_C]
        cums.append(jnp.where(lane < SSD_HEADS, _cumsum_rows(t, SSD_C, False), _cumsum_rows(t, SSD_C, True)))
    cum = jnp.concatenate(cums, axis=0)
    for d in range(2):
        for g in range(SSD_G):
            base = d * SSD_HEADS + g * SSD_HPG
            sh = (LANE - base) % LANE
            dt_ref[d, g] = dt if sh == 0 else pltpu.roll(dt, sh, 1)
            cum_ref[d, g] = cum if sh == 0 else pltpu.roll(cum, sh, 1)


def _dtprep(lay, proj, dt_bias, a_log):
    out = jax.ShapeDtypeStruct((2, SSD_G, lay.m, LANE), F32)
    ospec = pl.BlockSpec((2, SSD_G, RB, LANE), lambda i: (0, 0, i, 0))
    vspec = pl.BlockSpec((1, LANE), lambda i: (0, 0))
    return pl.pallas_call(
        _dtprep_kernel,
        out_shape=(out, out),
        grid=(lay.nb,),
        in_specs=[pl.BlockSpec((RB, LANE), lambda i: (i, C_DT // LANE)), vspec, vspec],
        out_specs=(ospec, ospec),
        compiler_params=_cparams(("arbitrary",)),
        name="dtprep",
    )(proj, dt_bias, a_log)


def _ssd_kernel(x_ref, b_ref, c_ref, dt_ref, cum_ref, h0_ref, dsk_ref, y_ref, hf_ref, h_ref, *, lay, reverse):
    i = pl.program_id(1)
    blk = (lay.nb - 1 - i) if reverse else i
    is_ctx, _, _, pos, bps = _seq_info(lay, blk)
    first = (pos == bps - 1) if reverse else (pos == 0)
    last = (pos == 0) if reverse else (pos == bps - 1)

    @pl.when(first & is_ctx)
    def _():
        h_ref[...] = jnp.zeros_like(h_ref)

    @pl.when(first & jnp.logical_not(is_ctx))
    def _():
        h_ref[...] = h0_ref[...]

    r = lax.broadcasted_iota(jnp.int32, (SSD_C, SSD_C), 0)
    c = lax.broadcasted_iota(jnp.int32, (SSD_C, SSD_C), 1)
    mask = (c >= r) if reverse else (c <= r)
    lo = c < SSD_P
    n_chunks = RB // SSD_C
    order = range(n_chunks - 1, -1, -1) if reverse else range(n_chunks)
    n_pairs = SSD_HPG // 2
    for ci in order:
        sl = slice(ci * SSD_C, (ci + 1) * SSD_C)
        x = x_ref[sl, :]
        bm = b_ref[sl, :].astype(BF16)
        cm = c_ref[sl, :].astype(BF16)
        dt = dt_ref[sl, :]
        cum = cum_ref[sl, :]
        cum_t = jnp.transpose(cum)
        cb = lax.dot_general(cm, bm, NT, preferred_element_type=F32)
        h = h_ref[...]
        y_inter = jnp.dot(cm, h.astype(BF16), preferred_element_type=F32)
        e_idx = 0 if reverse else SSD_C - 1
        ys, xdes, decs = [], [], []
        for j in range(n_pairs):
            cols, ls = [], []
            for rr in (2 * j, 2 * j + 1):
                col = jnp.broadcast_to(cum[:, rr:rr + 1], (SSD_C, SSD_C))
                seg = col - cum_t[rr:rr + 1, :]
                ls.append((jnp.exp(jnp.where(mask, seg, -1e30)) * cb).astype(BF16))
                cols.append(col)
            cp = jnp.where(lo, cols[0], cols[1])
            dtp = jnp.where(lo, jnp.broadcast_to(dt[:, 2 * j:2 * j + 1], (SSD_C, SSD_C)),
                            jnp.broadcast_to(dt[:, 2 * j + 1:2 * j + 2], (SSD_C, SSD_C)))
            xp = x[:, j * LANE:(j + 1) * LANE]
            xdt = xp * dtp
            rhs = jnp.concatenate([jnp.where(lo, xdt, 0.0), jnp.where(lo, 0.0, xdt)], axis=0).astype(BF16)
            lhs = jnp.concatenate(ls, axis=1)
            y = jnp.dot(lhs, rhs, preferred_element_type=F32)
            y = y + jnp.exp(cp) * y_inter[:, j * LANE:(j + 1) * LANE]
            if not reverse:
                y = y + dsk_ref[:, j * LANE:(j + 1) * LANE] * xp
            ys.append(y)
            edge = cp[e_idx:e_idx + 1, :]
            xdes.append((xdt * jnp.exp(edge - cp)).astype(BF16))
            decs.append(jnp.exp(edge))
        y_ref[sl, :] = jnp.concatenate(ys, axis=1)
        upd = lax.dot_general(bm, jnp.concatenate(xdes, axis=1), TN, preferred_element_type=F32)
        h_ref[...] = h * jnp.concatenate(decs, axis=1) + upd

    @pl.when(last & is_ctx)
    def _():
        hf_ref[...] = h_ref[...]


def _ssd_scan(lay, xbc, dtg, cumg, h0, dsk, *, reverse, name):
    d = 1 if reverse else 0
    blk = (lambda i: lay.nb - 1 - i) if reverse else (lambda i: i)
    cseq = lambda i: _seq_info(lay, blk(i))[1]
    lseq = lambda i: _seq_info(lay, blk(i))[2]
    gw = SSD_HPG * SSD_P
    return pl.pallas_call(
        functools.partial(_ssd_kernel, lay=lay, reverse=reverse),
        out_shape=(jax.ShapeDtypeStruct((lay.m, D_INNER), F32),
                   jax.ShapeDtypeStruct((lay.n_ctx, SSD_G, SSD_N, gw), F32)),
        grid=(SSD_G, lay.nb),
        in_specs=[
            pl.BlockSpec((RB, gw), lambda g, i: (blk(i), g)),
            pl.BlockSpec((RB, SSD_N), lambda g, i: (blk(i), D_INNER // SSD_N + g)),
            pl.BlockSpec((RB, SSD_N), lambda g, i: (blk(i), (D_INNER + SSD_BC) // SSD_N + g)),
            pl.BlockSpec((None, None, RB, LANE), lambda g, i: (d, g, blk(i), 0)),
            pl.BlockSpec((None, None, RB, LANE), lambda g, i: (d, g, blk(i), 0)),
            pl.BlockSpec((None, None, None, SSD_N, gw), lambda g, i: (lseq(i), d, g, 0, 0)),
            pl.BlockSpec((1, gw), lambda g, i: (0, g)),
        ],
        out_specs=(pl.BlockSpec((RB, gw), lambda g, i: (blk(i), g)),
                   pl.BlockSpec((None, None, SSD_N, gw), lambda g, i: (cseq(i), g, 0, 0))),
        scratch_shapes=[pltpu.VMEM((SSD_N, gw), F32)],
        compiler_params=_cparams(("arbitrary", "arbitrary")),
        name=name,
    )(xbc, xbc, xbc, dtg, cumg, h0, dsk)


def _split3(x):
    hi = x.astype(BF16)
    r1 = x - hi.astype(F32)
    mid = r1.astype(BF16)
    lo = (r1 - mid.astype(F32)).astype(BF16)
    return hi, mid, lo


def _norm_router_kernel(x_ref, gain_ref, scale_ref, shift_ref, wr_ref, h_ref, lg_ref):
    h = _rms(x_ref[...], gain_ref[...]) * (1.0 + scale_ref[...]) + shift_ref[...]
    h_ref[...] = h
    hs = _split3(h)
    ws = _split3(wr_ref[...])
    acc = jnp.zeros(lg_ref.shape, F32)
    for a, b in ((2, 0), (0, 2), (1, 1), (1, 0), (0, 1), (0, 0)):
        acc = acc + jnp.dot(hs[a], ws[b], preferred_element_type=F32)
    lg_ref[...] = acc


def _norm_router(lay, x, gain, mod3, wr):
    mrow = lambda i: _mod_row(lay, i * RB)
    vspec = lambda fn: pl.BlockSpec((None, 1, D), lambda i: (fn(i), 0, 0))
    return pl.pallas_call(
        _norm_router_kernel,
        out_shape=(jax.ShapeDtypeStruct((lay.m, D), F32), jax.ShapeDtypeStruct((lay.m, LANE), F32)),
        grid=(lay.nb,),
        in_specs=[pl.BlockSpec((RB, D), lambda i: (i, 0)), vspec(lambda i: 0),
                  vspec(lambda i: mrow(i) * 6 + 4), vspec(lambda i: mrow(i) * 6 + 3),
                  pl.BlockSpec((D, LANE), lambda i: (0, 0))],
        out_specs=(pl.BlockSpec((RB, D), lambda i: (i, 0)), pl.BlockSpec((RB, LANE), lambda i: (i, 0))),
        compiler_params=_cparams(("arbitrary",)),
        name="norm_router",
    )(x, gain, mod3, mod3, wr)


GATHER_UNROLL = 8


def _gather_rows(src_hbm, idx_ref, n, dst_row, sem, *, wait):
    def body(q, carry):
        cp = pltpu.make_async_copy(src_hbm.at[pl.ds(idx_ref[0, q], 1), :], dst_row(q), sem)
        if wait:
            cp.wait()
        else:
            cp.start()
        return carry

    lax.fori_loop(0, n, body, 0, unroll=GATHER_UNROLL)


def _expert_kernel(be_ref, nu_ref, rt_ref, rtn_ref, h_hbm, w1_ref, w3_ref, w2_ref, o_ref, xbuf, sem):
    b = pl.program_id(0)
    nu = nu_ref[0]

    @pl.when(b < nu)
    def _():
        slot = b % 2
        dst = lambda s: (lambda q: xbuf.at[s, pl.ds(q, 1), :])

        @pl.when(b == 0)
        def _():
            _gather_rows(h_hbm, rt_ref, MOE_BLOCK, dst(0), sem.at[0], wait=False)

        @pl.when(b + 1 < nu)
        def _():
            _gather_rows(h_hbm, rtn_ref, MOE_BLOCK, dst(1 - slot), sem.at[1 - slot], wait=False)

        _gather_rows(h_hbm, rt_ref, MOE_BLOCK, dst(slot), sem.at[slot], wait=True)
        x = xbuf[slot].astype(BF16)
        h1 = jnp.dot(x, w1_ref[...], preferred_element_type=F32)
        h3 = jnp.dot(x, w3_ref[...], preferred_element_type=F32)
        hid = (_silu(h1) * h3).astype(BF16)
        o_ref[...] = jnp.dot(hid, w2_ref[...], preferred_element_type=F32)

    @pl.when(b >= nu_ref[0])
    def _():
        o_ref[...] = jnp.zeros_like(o_ref)


def _experts(h2, row_tok, w1, w3, w2, block_e, n_used):
    n_rows = row_tok.shape[0]
    n_blocks = n_rows // MOE_BLOCK
    rt = row_tok.reshape(n_blocks, 1, MOE_BLOCK)
    wspec = lambda a, b: pl.BlockSpec((None, a, b), lambda i, be, nu: (be[i], 0, 0))
    ispec = lambda f: pl.BlockSpec((None, 1, MOE_BLOCK), lambda i, be, nu: (f(i), 0, 0), memory_space=pltpu.SMEM)
    return pl.pallas_call(
        _expert_kernel,
        out_shape=jax.ShapeDtypeStruct((n_rows, D), F32),
        grid_spec=pltpu.PrefetchScalarGridSpec(
            num_scalar_prefetch=2,
            grid=(n_blocks,),
            in_specs=[ispec(lambda i: i), ispec(lambda i: jnp.minimum(i + 1, n_blocks - 1)),
                      pl.BlockSpec(memory_space=pl.ANY),
                      wspec(D, D_EXP), wspec(D, D_EXP), wspec(D_EXP, D)],
            out_specs=pl.BlockSpec((MOE_BLOCK, D), lambda i, be, nu: (i, 0)),
            scratch_shapes=[pltpu.VMEM((2, MOE_BLOCK, D), F32), pltpu.SemaphoreType.DMA((2,))],
        ),
        compiler_params=_cparams(("arbitrary",)),
        name="experts",
    )(block_e, n_used, rt, rt, h2, w1, w3, w2)


def _combine_kernel(dc_ref, dn_ref, x_ref, tw_ref, gate_ref, nf_ref, yb_hbm, *rest, lay, final):
    if final:
        op_ref, os_ref, ybuf, sem = rest
    else:
        o_ref, ybuf, sem = rest
    i = pl.program_id(0)
    slot = i % 2
    n_q = RB * TOP_K
    assert TOP_K == 2
    dst = lambda s: (lambda q: ybuf.at[s, q & 1, pl.ds(lax.shift_right_logical(q, 1), 1), :])

    @pl.when(i == 0)
    def _():
        _gather_rows(yb_hbm, dc_ref, n_q, dst(0), sem.at[0], wait=False)

    @pl.when(i + 1 < lay.nb)
    def _():
        _gather_rows(yb_hbm, dn_ref, n_q, dst(1 - slot), sem.at[1 - slot], wait=False)

    _gather_rows(yb_hbm, dc_ref, n_q, dst(slot), sem.at[slot], wait=True)
    tw = tw_ref[...]
    y = x_ref[...] + gate_ref[...] * (ybuf[slot, 0] * tw[:, 0:1] + ybuf[slot, 1] * tw[:, 1:2])
    if not final:
        o_ref[...] = y
    else:
        y = _rms(y, nf_ref[...])

        @pl.when(i < lay.nb_ctx)
        def _():
            op_ref[...] = y

        @pl.when(i >= lay.nb_ctx)
        def _():
            os_ref[...] = y


def _combine(lay, x, yb, dest, tw, mod3, norm_f, *, final):
    mrow = lambda i: _mod_row(lay, i * RB)
    spec = pl.BlockSpec((RB, D), lambda i: (i, 0))
    n_q = RB * TOP_K
    dq = dest.reshape(lay.nb, 1, n_q)
    ispec = lambda f: pl.BlockSpec((None, 1, n_q), lambda i: (f(i), 0, 0), memory_space=pltpu.SMEM)
    if final:
        out_shape = (jax.ShapeDtypeStruct((lay.m_ctx, D), F32), jax.ShapeDtypeStruct((lay.m - lay.m_ctx, D), F32))
        out_specs = (pl.BlockSpec((RB, D), lambda i: (jnp.minimum(i, lay.nb_ctx - 1), 0)),
                     pl.BlockSpec((RB, D), lambda i: (jnp.maximum(i - lay.nb_ctx, 0), 0)))
    else:
        out_shape, out_specs = jax.ShapeDtypeStruct((lay.m, D), F32), spec
    return pl.pallas_call(
        functools.partial(_combine_kernel, lay=lay, final=final),
        out_shape=out_shape,
        grid=(lay.nb,),
        in_specs=[ispec(lambda i: i), ispec(lambda i: jnp.minimum(i + 1, lay.nb - 1)), spec,
                  pl.BlockSpec((RB, TOP_K), lambda i: (i, 0)),
                  pl.BlockSpec((None, 1, D), lambda i: (mrow(i) * 6 + 5, 0, 0)),
                  pl.BlockSpec((1, D), lambda i: (0, 0)),
                  pl.BlockSpec(memory_space=pl.ANY)],
        out_specs=out_specs,
        scratch_shapes=[pltpu.VMEM((2, TOP_K, RB, D), F32), pltpu.SemaphoreType.DMA((2,))],
        compiler_params=_cparams(("arbitrary",)),
        name="final" if final else "combine",
    )(dq, dq, x, tw, mod3, norm_f, yb)


def _route(logits, router_bias, n_tok):
    n_asg = n_tok * TOP_K
    scores = jax.nn.sigmoid(logits)
    biased = scores + router_bias.astype(F32)
    bg = biased.reshape(n_tok, N_GRP, EPG)
    pair_sums = [bg[..., a] + bg[..., b] for a in range(EPG) for b in range(a + 1, EPG)]
    group_score = functools.reduce(jnp.maximum, pair_sums)
    best_group = jnp.argmax(group_score, axis=-1)
    sel = (jnp.arange(N_GRP)[None, :, None] == best_group[:, None, None])
    vb = jnp.sum(jnp.where(sel, bg, 0.0), axis=1)
    vs = jnp.sum(jnp.where(sel, scores.reshape(n_tok, N_GRP, EPG), 0.0), axis=1)
    lane = jnp.arange(EPG)[None, :]
    i1 = jnp.argmax(vb, axis=-1)
    i2 = jnp.argmax(jnp.where(lane == i1[:, None], -jnp.inf, vb), axis=-1)
    top_loc = jnp.stack([i1, i2], axis=-1)
    top_idx = (best_group[:, None] * EPG + top_loc).astype(jnp.int32)
    top_w = jnp.sum(jnp.where(lane[:, None, :] == top_loc[:, :, None], vs[:, None, :], 0.0), axis=-1)
    top_w = top_w / jnp.sum(top_w, axis=-1, keepdims=True)
    e_flat = top_idx.reshape(-1)
    tok_flat = jnp.arange(n_asg, dtype=jnp.int32) // TOP_K
    cb = MOE_BLOCK
    oh = (e_flat[:, None] == jnp.arange(N_EXP, dtype=e_flat.dtype)[None, :]).astype(F32).reshape(n_asg // cb, cb, N_EXP)
    tri = (jnp.arange(cb)[:, None] > jnp.arange(cb)[None, :]).astype(F32)
    within = jnp.einsum('ts,bse->bte', tri, oh)
    bsum = oh.sum(1)
    counts = bsum.sum(0).astype(jnp.int32)
    padded = (counts + MOE_BLOCK - 1) // MOE_BLOCK * MOE_BLOCK
    padded_end = jnp.cumsum(padded)
    padded_start = padded_end - padded
    base = jnp.cumsum(bsum, axis=0) - bsum + padded_start.astype(F32)[None, :]
    dest = ((within + base[:, None, :]) * oh).sum(-1).reshape(n_asg).astype(jnp.int32)
    n_blocks = -(-n_asg // MOE_BLOCK) + N_EXP
    n_rows = n_blocks * MOE_BLOCK
    row_tok = jnp.zeros((n_rows,), jnp.int32).at[dest].set(tok_flat)
    block_start = jnp.arange(n_blocks, dtype=jnp.int32) * MOE_BLOCK
    block_e = jnp.minimum(jnp.searchsorted(padded_end, block_start, side='right'), N_EXP - 1).astype(jnp.int32)
    n_used = (padded_end[-1] // MOE_BLOCK).astype(jnp.int32).reshape(1)
    return row_tok, top_w, block_e, n_used, dest.reshape(n_tok, TOP_K)


def _sincos_2d(rows, cols, dim):
    quarter = dim // 4
    omega = 1.0 / (POS_BASE ** (jnp.arange(quarter, dtype=F32) / quarter))

    def axis_embed(n):
        ang = jnp.arange(n, dtype=F32)[:, None] * omega[None, :]
        return jnp.concatenate([jnp.sin(ang), jnp.cos(ang)], axis=-1)

    er = jnp.broadcast_to(axis_embed(rows)[:, None, :], (rows, cols, dim // 2))
    ec = jnp.broadcast_to(axis_embed(cols)[None, :, :], (rows, cols, dim // 2))
    return jnp.concatenate([er, ec], axis=-1).reshape(rows * cols, dim)


def _rearranged_w_in(w):
    sizes = (GLA_QK, GLA_QK, GLA_V, GLA_V, 2 * GLA_LR, D_INNER, CONV_DIM, 2 * SSD_HEADS, D, D)
    offs = [0]
    for s in sizes:
        offs.append(offs[-1] + s)
    q, k, v, g, dlr, z, xbc, dt, ga, gb = [w[:, offs[i]:offs[i + 1]] for i in range(len(sizes))]
    zpad = jnp.zeros((w.shape[0], LANE - GLA_LR), w.dtype)
    return jnp.concatenate([q, k, v, g, ga, z, xbc, gb, dt, dlr[:, :GLA_LR], zpad, dlr[:, GLA_LR:], zpad,
                            jnp.zeros((w.shape[0], LANE), w.dtype)], axis=1).astype(BF16)


def _pick_tile(lay, want):
    t = want
    while lay.m_ctx % t or lay.t_lat % t:
        t //= 2
    return t


def kernel(x_prompt, x_sample, c, state_gla, state_ssd, c_ctx, w_ada, b_ada, norm1, norm2, w_in, gla_dw2, gla_db, gla_norm, w_gla_b, conv_w, conv_b, a_log, dt_bias, d_skip, ssd_norm, w_ssd_b, w_out, w_router, router_bias, w_e1, w_e3, w_e2, norm_f):
    n_ctx, t_ctx, _ = x_prompt.shape
    n_lat, t_lat, _ = x_sample.shape
    lay = Layout(n_ctx, t_ctx, n_lat, t_lat)
    depth = w_in.shape[0]
    m = lay.m
    assert t_ctx % RB == 0 and t_lat % RB == 0 and t_lat % GRID_W == 0

    pos = _sincos_2d(t_lat // GRID_W, GRID_W, D)
    x = _embed(lay, x_prompt.reshape(lay.m_ctx, D), x_sample.reshape(n_lat * t_lat, D), pos)

    n_mod = 16
    cond = jnp.concatenate([c_ctx[None, :], c, jnp.zeros((n_mod - 1 - n_lat, D), F32)], axis=0)
    wr = jnp.pad(w_router, ((0, 0), (0, LANE - N_EXP)))

    tm_in = _pick_tile(lay, 1024)
    tm_a = _pick_tile(lay, 512)
    mrow = lambda tm: (lambda i: _mod_row(lay, i * tm))
    new_gla, new_ssd = [], []
    y_prompt = y_sample = None
    for l in range(depth):
        mod = _fused_mm("ada", n_mod, D, 6 * D, tm=n_mod, tn=1024, rows=[(cond, D, 0)], vecs=[],
                        w=w_ada[l].astype(BF16), evecs=[(b_ada[l].reshape(1, 1, 6 * D), lambda i: 0)],
                        prologue=_pro_silu, epilogue=_epi_bias, out_dtype=F32, rc=n_mod)
        mod3 = mod.reshape(n_mod * 6, 1, D)
        n1 = norm1[l].reshape(1, 1, D)
        mr = mrow(tm_in)
        proj = _fused_mm("in_proj", m, D, NP, tm=tm_in, tn=512, rows=[(x, D, 0)],
                         vecs=[(n1, D, lambda i: 0), (mod3, D, lambda i: mr(i) * 6 + 1),
                               (mod3, D, lambda i: mr(i) * 6 + 0)],
                         w=_rearranged_w_in(w_in[l]), prologue=_pro_norm_mod, epilogue=None, out_dtype=F32)

        dw2p = jnp.pad(gla_dw2[l], ((0, 0), (0, LANE - GLA_LR), (0, 0))).astype(BF16)
        db = gla_db[l].reshape(2, 1, GLA_QK)
        o_f, sg_f = _gla_scan(lay, proj, dw2p, db, state_gla, l, reverse=False, name="gla_fwd")
        o_b, sg_b = _gla_scan(lay, proj, dw2p, db, state_gla, l, reverse=True, name="gla_bwd")
        new_gla.append(jnp.stack([sg_f, sg_b], axis=1))
        m_a = _kmm("gla_out", m, GLA_V, D, tm=tm_a, tk=GLA_DV,
                   rows=[(o_f, 0), (o_b, 0), (proj, C_G // GLA_DV)], vecs=[gla_norm[l].reshape(1, GLA_V)],
                   w=w_gla_b[l].astype(BF16), erows=[(proj, C_GA // D)],
                   prologue=_pro_gla_out, epilogue=_epi_gate_a, out_dtype=F32)

        xbc = _conv(lay, proj, conv_w[l], conv_b[l].reshape(1, CONV_DIM))
        dtg, cumg = _dtprep(lay, proj, dt_bias[l].reshape(1, LANE), a_log[l].reshape(1, LANE))
        h0 = state_ssd[:, l].reshape(n_lat, 2, SSD_G, SSD_HPG, SSD_N, SSD_P)
        h0 = h0.transpose(0, 1, 2, 4, 3, 5).reshape(n_lat, 2, SSD_G, SSD_N, SSD_HPG * SSD_P)
        dsk = jnp.repeat(d_skip[l], SSD_P).reshape(1, D_INNER)
        y_f, hs_f = _ssd_scan(lay, xbc, dtg, cumg, h0, dsk, reverse=False, name="ssd_fwd")
        y_b, hs_b = _ssd_scan(lay, xbc, dtg, cumg, h0, dsk, reverse=True, name="ssd_bwd")
        hs = jnp.stack([hs_f, hs_b], axis=1).reshape(n_ctx, 2, SSD_G, SSD_N, SSD_HPG, SSD_P)
        new_ssd.append(hs.transpose(0, 1, 2, 4, 3, 5).reshape(n_ctx, 2, SSD_HEADS, SSD_N, SSD_P))
        merged = _kmm("ssd_out", m, D_INNER, D, tm=tm_a, tk=512,
                      rows=[(y_f, 0), (y_b, 0), (proj, C_Z // 512)], vecs=[ssd_norm[l].reshape(1, D_INNER)],
                      w=w_ssd_b[l].astype(BF16), erows=[(m_a, 0), (proj, C_GB // D)],
                      prologue=_pro_ssd_out, epilogue=_epi_gate_b, out_dtype=BF16)

        x = _fused_mm("mix_out", m, D, D, tm=tm_in, tn=512, rows=[(merged, D, 0)], vecs=[],
                      w=w_out[l].astype(BF16), erows=[(x, 0)], evecs=[(mod3, lambda i: mr(i) * 6 + 2)],
                      prologue=None, epilogue=_epi_resid, out_dtype=F32)

        h2, logits = _norm_router(lay, x, norm2[l].reshape(1, 1, D), mod3, wr)
        row_tok, top_w, block_e, n_used, pos_tk = _route(logits[:, :N_EXP], router_bias, m)
        yb = _experts(h2, row_tok, w_e1[l].astype(BF16), w_e3[l].astype(BF16), w_e2[l].astype(BF16),
                      block_e, n_used)
        if l + 1 < depth:
            x = _combine(lay, x, yb, pos_tk, top_w, mod3, norm_f.reshape(1, D), final=False)
        else:
            y_prompt, y_sample = _combine(lay, x, yb, pos_tk, top_w, mod3, norm_f.reshape(1, D), final=True)

    return (y_prompt.reshape(n_ctx, t_ctx, D), y_sample.reshape(n_lat, t_lat, D),
            jnp.stack(new_gla, axis=1), jnp.stack(new_ssd, axis=1))
```

```python
import functools
import math
from typing import NamedTuple

import jax
import jax.numpy as jnp
from jax import lax
from jax.experimental import pallas as pl
from jax.experimental.pallas import tpu as pltpu

F32 = jnp.float32
BF16 = jnp.bfloat16

D = 2048
GRID_W = 64
POS_BASE = 10000.0
EPS = 1e-6
GLA_H = 4
GLA_DK = D // 2 // GLA_H
GLA_DV = D // GLA_H
GLA_QK = GLA_H * GLA_DK
GLA_V = GLA_H * GLA_DV
GLA_LR = 16
GLA_TAU = 16.0
GLA_C = 64
D_INNER = 2 * D
SSD_P = 64
SSD_HEADS = D_INNER // SSD_P
SSD_N = 128
SSD_G = 8
SSD_HPG = SSD_HEADS // SSD_G
SSD_BC = SSD_G * SSD_N
CONV_DIM = D_INNER + 2 * SSD_BC
D_CONV = 5
SSD_C = 128
N_EXP = 16
N_GRP = 4
EPG = N_EXP // N_GRP
TOP_K = 2
D_EXP = D // 2
MOE_BLOCK = 256

LANE = 128
SUBLANE = 8
RB = 256
VMEM_LIMIT = 56 * 1024 * 1024

C_Q = 0
C_K = C_Q + GLA_QK
C_V = C_K + GLA_QK
C_G = C_V + GLA_V
C_GA = C_G + GLA_V
C_Z = C_GA + D
C_XBC = C_Z + D_INNER
C_B = C_XBC + D_INNER
C_CM = C_B + SSD_BC
C_GB = C_XBC + CONV_DIM
C_DT = C_GB + D
C_DLR = C_DT + 2 * SSD_HEADS
NP = C_DLR + 3 * LANE


class Layout(NamedTuple):
    n_ctx: int
    t_ctx: int
    n_lat: int
    t_lat: int

    @property
    def m_ctx(self):
        return self.n_ctx * self.t_ctx

    @property
    def m(self):
        return self.m_ctx + self.n_lat * self.t_lat

    @property
    def nb(self):
        return self.m // RB

    @property
    def nb_ctx(self):
        return self.m_ctx // RB

    @property
    def bps_ctx(self):
        return self.t_ctx // RB

    @property
    def bps_lat(self):
        return self.t_lat // RB


def _seq_info(lay, blk):
    is_ctx = blk < lay.nb_ctx
    lb = jnp.maximum(blk - lay.nb_ctx, 0)
    cb = jnp.minimum(blk, lay.nb_ctx - 1)
    pos = jnp.where(is_ctx, cb % lay.bps_ctx, lb % lay.bps_lat)
    bps = jnp.where(is_ctx, lay.bps_ctx, lay.bps_lat)
    return is_ctx, cb // lay.bps_ctx, jnp.minimum(lb // lay.bps_lat, lay.n_lat - 1), pos, bps


def _mod_row(lay, row_start):
    return jnp.where(row_start < lay.m_ctx, 0, 1 + jnp.maximum(row_start - lay.m_ctx, 0) // lay.t_lat)


def _cparams(sem):
    return pltpu.CompilerParams(dimension_semantics=sem, vmem_limit_bytes=VMEM_LIMIT)


def _silu(x):
    return x * jax.nn.sigmoid(x)


def _softplus(x):
    return jnp.maximum(x, 0.0) + jnp.log1p(jnp.exp(-jnp.abs(x)))


def _cumsum_rows(x, n, reverse):
    rows = lax.broadcasted_iota(jnp.int32, x.shape, 0)
    s = 1
    while s < n:
        if reverse:
            x = x + jnp.where(rows < n - s, pltpu.roll(x, n - s, 0), 0.0)
        else:
            x = x + jnp.where(rows >= s, pltpu.roll(x, s, 0), 0.0)
        s *= 2
    return x


NT = (((1,), (1,)), ((), ()))
TN = (((0,), (0,)), ((), ()))


def _embed_kernel(xp_ref, xs_ref, pos_ref, o_ref, *, lay):
    rb = pl.program_id(0)

    @pl.when(rb < lay.nb_ctx)
    def _():
        o_ref[...] = xp_ref[...]

    @pl.when(rb >= lay.nb_ctx)
    def _():
        o_ref[...] = xs_ref[...] + pos_ref[...]


def _embed(lay, xp, xs, pos):
    spec = lambda f: pl.BlockSpec((RB, D), f)
    return pl.pallas_call(
        functools.partial(_embed_kernel, lay=lay),
        out_shape=jax.ShapeDtypeStruct((lay.m, D), F32),
        grid=(lay.nb,),
        in_specs=[spec(lambda i: (jnp.minimum(i, lay.nb_ctx - 1), 0)),
                  spec(lambda i: (jnp.maximum(i - lay.nb_ctx, 0), 0)),
                  spec(lambda i: (jnp.maximum(i - lay.nb_ctx, 0) % lay.bps_lat, 0))],
        out_specs=spec(lambda i: (i, 0)),
        compiler_params=_cparams(("arbitrary",)),
        name="embed",
    )(xp, xs, pos)


def _fused_mm_kernel(*refs, n_rows, n_vecs, n_erows, n_evecs, prologue, epilogue, tm, rc):
    rows = refs[:n_rows]
    vecs = refs[n_rows:n_rows + n_vecs]
    w_ref = refs[n_rows + n_vecs]
    p = n_rows + n_vecs + 1
    erows = refs[p:p + n_erows]
    evecs = refs[p + n_erows:p + n_erows + n_evecs]
    o_ref = refs[p + n_erows + n_evecs]

    if prologue is None:
        lhs = rows[0][...]
    else:
        h_ref = refs[p + n_erows + n_evecs + 1]

        @pl.when(pl.program_id(1) == 0)
        def _():
            vs = [v[...] for v in vecs]

            def chunk(c, carry):
                sl = pl.ds(pl.multiple_of(c * rc, rc), rc)
                h_ref[sl, :] = prologue([r[sl, :] for r in rows], vs).astype(BF16)
                return carry

            lax.fori_loop(0, tm // rc, chunk, 0)

        lhs = h_ref[...]
    acc = jnp.dot(lhs, w_ref[...], preferred_element_type=F32)
    if epilogue is not None:
        acc = epilogue(acc, [e[...] for e in erows], [v[...] for v in evecs])
    o_ref[...] = acc.astype(o_ref.dtype)


def _fused_mm(name, m, k, n, *, tm, tn, rows, vecs, w, w_col0=0, erows=(), evecs=(), prologue, epilogue,
              out_dtype, rc=32):
    in_specs, args = [], []
    for arr, width, cb in rows:
        in_specs.append(pl.BlockSpec((tm, width), lambda i, j, cb=cb: (i, cb)))
        args.append(arr)
    for arr, width, fn in vecs:
        in_specs.append(pl.BlockSpec((None, 1, width), lambda i, j, fn=fn: (fn(i), 0, 0)))
        args.append(arr)
    in_specs.append(pl.BlockSpec((k, tn), lambda i, j: (0, w_col0 + j)))
    args.append(w)
    for arr, cb0 in erows:
        in_specs.append(pl.BlockSpec((tm, tn), lambda i, j, cb0=cb0: (i, cb0 + j)))
        args.append(arr)
    for arr, fn in evecs:
        in_specs.append(pl.BlockSpec((None, 1, tn), lambda i, j, fn=fn: (fn(i), 0, j)))
        args.append(arr)
    scratch = [] if prologue is None else [pltpu.VMEM((tm, k), BF16)]
    kern = functools.partial(_fused_mm_kernel, n_rows=len(rows), n_vecs=len(vecs), n_erows=len(erows),
                             n_evecs=len(evecs), prologue=prologue, epilogue=epilogue, tm=tm, rc=rc)
    return pl.pallas_call(
        kern,
        out_shape=jax.ShapeDtypeStruct((m, n), out_dtype),
        grid=(m // tm, n // tn),
        in_specs=in_specs,
        out_specs=pl.BlockSpec((tm, tn), lambda i, j: (i, j)),
        scratch_shapes=scratch,
        compiler_params=_cparams(("arbitrary", "arbitrary")),
        name=name,
    )(*args)


def _rms(x, gain):
    return x * lax.rsqrt(jnp.mean(x * x, axis=-1, keepdims=True) + EPS) * gain


def _pro_norm_mod(rows, vecs):
    (x,), (gain, scale, shift) = rows, vecs
    return _rms(x, gain) * (1.0 + scale) + shift


def _pro_silu(rows, vecs):
    return _silu(rows[0])


def _epi_bias(acc, erows, evecs):
    return acc + evecs[0]


def _epi_resid(acc, erows, evecs):
    return erows[0] + evecs[0] * acc


def _kmm_kernel(*refs, n_rows, n_vecs, n_erows, prologue, epilogue, tm, rc, nk):
    rows = refs[:n_rows]
    vecs = refs[n_rows:n_rows + n_vecs]
    w_ref = refs[n_rows + n_vecs]
    p = n_rows + n_vecs + 1
    erows = refs[p:p + n_erows]
    o_ref, acc_ref, lhs_ref, ss_ref = refs[p + n_erows:p + n_erows + 4]
    k = pl.program_id(1)

    @pl.when(k == 0)
    def _():
        acc_ref[...] = jnp.zeros_like(acc_ref)
        ss_ref[...] = jnp.zeros_like(ss_ref)

    vs = [v[...] for v in vecs]

    def chunk(c, carry):
        sl = pl.ds(pl.multiple_of(c * rc, rc), rc)
        lhs, ss = prologue([r[sl, :] for r in rows], vs)
        lhs_ref[sl, :] = lhs.astype(BF16)
        if ss is not None:
            ss_ref[sl, :] += ss
        return carry

    lax.fori_loop(0, tm // rc, chunk, 0)
    acc_ref[...] += jnp.dot(lhs_ref[...], w_ref[...], preferred_element_type=F32)

    @pl.when(k == nk - 1)
    def _():
        o_ref[...] = epilogue(acc_ref[...], ss_ref[...], [e[...] for e in erows]).astype(o_ref.dtype)


def _kmm(name, m, k, n, *, tm, tk, rows, vecs, w, erows, prologue, epilogue, out_dtype, rc=64):
    in_specs, args = [], []
    for arr, cb0 in rows:
        in_specs.append(pl.BlockSpec((tm, tk), lambda i, kk, cb0=cb0: (i, cb0 + kk)))
        args.append(arr)
    for arr in vecs:
        in_specs.append(pl.BlockSpec((1, tk), lambda i, kk: (0, kk)))
        args.append(arr)
    in_specs.append(pl.BlockSpec((tk, n), lambda i, kk: (kk, 0)))
    args.append(w)
    for arr, cb in erows:
        in_specs.append(pl.BlockSpec((tm, n), lambda i, kk, cb=cb: (i, cb)))
        args.append(arr)
    kern = functools.partial(_kmm_kernel, n_rows=len(rows), n_vecs=len(vecs), n_erows=len(erows),
                             prologue=prologue, epilogue=epilogue, tm=tm, rc=rc, nk=k // tk)
    return pl.pallas_call(
        kern,
        out_shape=jax.ShapeDtypeStruct((m, n), out_dtype),
        grid=(m // tm, k // tk),
        in_specs=in_specs,
        out_specs=pl.BlockSpec((tm, n), lambda i, kk: (i, 0)),
        scratch_shapes=[pltpu.VMEM((tm, n), F32), pltpu.VMEM((tm, tk), BF16), pltpu.VMEM((tm, 1), F32)],
        compiler_params=_cparams(("arbitrary", "arbitrary")),
        name=name,
    )(*args)


def _pro_gla_out(rows, vecs):
    (of, ob, g), (gain,) = rows, vecs
    return _rms(of + ob, gain) * _silu(g), None


def _epi_gate_a(acc, ss, erows):
    return jax.nn.sigmoid(erows[0]) * acc


def _pro_ssd_out(rows, vecs):
    (yf, yb, z), (gain,) = rows, vecs
    u = (yf + yb) * _silu(z)
    return u * gain, jnp.sum(u * u, axis=-1, keepdims=True)


def _epi_gate_b(acc, ss, erows):
    rstd = lax.rsqrt(ss * (1.0 / D_INNER) + EPS)
    return erows[0] + jax.nn.sigmoid(erows[1]) * (acc * rstd)


def _gla_kernel(q_ref, k_ref, v_ref, dlr_ref, dw_ref, db_ref, s0_ref, o_ref, sf_ref, s_ref, *, lay, reverse):
    i = pl.program_id(1)
    blk = (lay.nb - 1 - i) if reverse else i
    is_ctx, _, _, pos, bps = _seq_info(lay, blk)
    first = (pos == bps - 1) if reverse else (pos == 0)
    last = (pos == 0) if reverse else (pos == bps - 1)

    @pl.when(first & is_ctx)
    def _():
        s_ref[...] = jnp.zeros_like(s_ref)

    @pl.when(first & jnp.logical_not(is_ctx))
    def _():
        s_ref[...] = s0_ref[...]

    x = jnp.dot(dlr_ref[...].astype(BF16), dw_ref[...], preferred_element_type=F32) + db_ref[...]
    log_a = (jnp.minimum(x, 0.0) - jnp.log1p(jnp.exp(-jnp.abs(x)))) * (1.0 / GLA_TAU)

    r = lax.broadcasted_iota(jnp.int32, (GLA_C, GLA_C), 0)
    c = lax.broadcasted_iota(jnp.int32, (GLA_C, GLA_C), 1)
    mask = (c >= r) if reverse else (c <= r)
    n_chunks = RB // GLA_C
    order = range(n_chunks - 1, -1, -1) if reverse else range(n_chunks)
    for ci in order:
        sl = slice(ci * GLA_C, (ci + 1) * GLA_C)
        b = _cumsum_rows(log_a[sl], GLA_C, reverse)
        q = q_ref[sl, :] * (GLA_DK ** -0.5)
        k = k_ref[sl, :]
        v = v_ref[sl, :].astype(BF16)
        qe = (q * jnp.exp(b)).astype(BF16)
        ke = (k * jnp.exp(-b)).astype(BF16)
        att = jnp.where(mask, lax.dot_general(qe, ke, NT, preferred_element_type=F32), 0.0)
        s = s_ref[...]
        o = jnp.dot(att.astype(BF16), v, preferred_element_type=F32)
        o = o + jnp.dot(qe, s.astype(BF16), preferred_element_type=F32)
        o_ref[sl, :] = o
        b_end = b[0:1, :] if reverse else b[GLA_C - 1:GLA_C, :]
        kd = (k * jnp.exp(b_end - b)).astype(BF16)
        e_col = jnp.exp(jnp.transpose(jnp.broadcast_to(b_end, (LANE, GLA_DK))))
        upd = lax.dot_general(kd, v, TN, preferred_element_type=F32)
        s_ref[...] = s * jnp.concatenate([e_col] * (GLA_DV // LANE), axis=1) + upd

    @pl.when(last & is_ctx)
    def _():
        sf_ref[...] = s_ref[...]


def _gla_scan(lay, proj, dw2p, db, s0, layer, *, reverse, name):
    d = 1 if reverse else 0
    blk = (lambda i: lay.nb - 1 - i) if reverse else (lambda i: i)
    cseq = lambda i: _seq_info(lay, blk(i))[1]
    lseq = lambda i: _seq_info(lay, blk(i))[2]
    return pl.pallas_call(
        functools.partial(_gla_kernel, lay=lay, reverse=reverse),
        out_shape=(jax.ShapeDtypeStruct((lay.m, GLA_V), F32),
                   jax.ShapeDtypeStruct((lay.n_ctx, GLA_H, GLA_DK, GLA_DV), F32)),
        grid=(GLA_H, lay.nb),
        in_specs=[
            pl.BlockSpec((RB, GLA_DK), lambda h, i: (blk(i), C_Q // GLA_DK + h)),
            pl.BlockSpec((RB, GLA_DK), lambda h, i: (blk(i), C_K // GLA_DK + h)),
            pl.BlockSpec((RB, GLA_DV), lambda h, i: (blk(i), C_V // GLA_DV + h)),
            pl.BlockSpec((RB, LANE), lambda h, i: (blk(i), C_DLR // LANE + d)),
            pl.BlockSpec((None, LANE, GLA_DK), lambda h, i: (d, 0, h)),
            pl.BlockSpec((None, 1, GLA_DK), lambda h, i: (d, 0, h)),
            pl.BlockSpec((None, None, None, None, GLA_DK, GLA_DV), lambda h, i: (lseq(i), layer, d, h, 0, 0)),
        ],
        out_specs=(pl.BlockSpec((RB, GLA_DV), lambda h, i: (blk(i), h)),
                   pl.BlockSpec((None, None, GLA_DK, GLA_DV), lambda h, i: (cseq(i), h, 0, 0))),
        scratch_shapes=[pltpu.VMEM((GLA_DK, GLA_DV), F32)],
        compiler_params=_cparams(("arbitrary", "arbitrary")),
        name=name,
    )(proj, proj, proj, proj, dw2p, db, s0)


CONV_TC = 2048
CONV_SC = 512


def _conv_kernel(prev_ref, cur_ref, next_ref, w_ref, b_ref, o_ref, *, lay):
    rb = pl.program_id(0)
    _, _, _, pos, bps = _seq_info(lay, rb)
    n = RB + 2 * SUBLANE
    for s in range(CONV_TC // CONV_SC):
        cs = slice(s * CONV_SC, (s + 1) * CONV_SC)
        prev = jnp.where(pos > 0, prev_ref[:, cs], 0.0)
        nxt = jnp.where(pos < bps - 1, next_ref[:, cs], 0.0)
        ext = jnp.concatenate([prev, cur_ref[:, cs], nxt], axis=0)
        w = w_ref[:, cs]
        acc = jnp.broadcast_to(b_ref[:, cs], (RB, CONV_SC))
        for j in range(D_CONV):
            off = j - D_CONV // 2
            sh = ext if off == 0 else pltpu.roll(ext, (-off) % n, 0)
            acc = acc + w[j:j + 1, :] * sh[SUBLANE:SUBLANE + RB, :]
        o_ref[:, cs] = _silu(acc)


def _conv(lay, proj, cw, cb):
    c0 = C_XBC // CONV_TC
    rpb = RB // SUBLANE
    return pl.pallas_call(
        functools.partial(_conv_kernel, lay=lay),
        out_shape=jax.ShapeDtypeStruct((lay.m, CONV_DIM), F32),
        grid=(lay.nb, CONV_DIM // CONV_TC),
        in_specs=[
            pl.BlockSpec((SUBLANE, CONV_TC), lambda i, j: (jnp.maximum(i * rpb - 1, 0), c0 + j)),
            pl.BlockSpec((RB, CONV_TC), lambda i, j: (i, c0 + j)),
            pl.BlockSpec((SUBLANE, CONV_TC), lambda i, j: (jnp.minimum((i + 1) * rpb, lay.m // SUBLANE - 1), c0 + j)),
            pl.BlockSpec((D_CONV, CONV_TC), lambda i, j: (0, j)),
            pl.BlockSpec((1, CONV_TC), lambda i, j: (0, j)),
        ],
        out_specs=pl.BlockSpec((RB, CONV_TC), lambda i, j: (i, j)),
        compiler_params=_cparams(("arbitrary", "arbitrary")),
        name="conv",
    )(proj, proj, proj, cw, cb)


def _dtprep_kernel(raw_ref, bias_ref, alog_ref, dt_ref, cum_ref):
    dt = _softplus(raw_ref[...] + bias_ref[...])
    dta = dt * (-jnp.exp(alog_ref[...]))
    lane = lax.broadcasted_iota(jnp.int32, (SSD_C, LANE), 1)
    cums = []
    for ci in range(RB // SSD_C):
        t = dta[ci * SSD_C:(ci + 1) * SSD_C]
        cums.append(jnp.where(lane < SSD_HEADS, _cumsum_rows(t, SSD_C, False), _cumsum_rows(t, SSD_C, True)))
    cum = jnp.concatenate(cums, axis=0)
    for d in range(2):
        for g in range(SSD_G):
            base = d * SSD_HEADS + g * SSD_HPG
            sh = (LANE - base) % LANE
            dt_ref[d, g] = dt if sh == 0 else pltpu.roll(dt, sh, 1)
            cum_ref[d, g] = cum if sh == 0 else pltpu.roll(cum, sh, 1)


def _dtprep(lay, proj, dt_bias, a_log):
    out = jax.ShapeDtypeStruct((2, SSD_G, lay.m, LANE), F32)
    ospec = pl.BlockSpec((2, SSD_G, RB, LANE), lambda i: (0, 0, i, 0))
    vspec = pl.BlockSpec((1, LANE), lambda i: (0, 0))
    return pl.pallas_call(
        _dtprep_kernel,
        out_shape=(out, out),
        grid=(lay.nb,),
        in_specs=[pl.BlockSpec((RB, LANE), lambda i: (i, C_DT // LANE)), vspec, vspec],
        out_specs=(ospec, ospec),
        compiler_params=_cparams(("arbitrary",)),
        name="dtprep",
    )(proj, dt_bias, a_log)


def _ssd_kernel(x_ref, b_ref, c_ref, dt_ref, cum_ref, h0_ref, dsk_ref, y_ref, hf_ref, h_ref, *, lay, reverse):
    i = pl.program_id(1)
    blk = (lay.nb - 1 - i) if reverse else i
    is_ctx, _, _, pos, bps = _seq_info(lay, blk)
    first = (pos == bps - 1) if reverse else (pos == 0)
    last = (pos == 0) if reverse else (pos == bps - 1)

    @pl.when(first & is_ctx)
    def _():
        h_ref[...] = jnp.zeros_like(h_ref)

    @pl.when(first & jnp.logical_not(is_ctx))
    def _():
        h_ref[...] = h0_ref[...]

    r = lax.broadcasted_iota(jnp.int32, (SSD_C, SSD_C), 0)
    c = lax.broadcasted_iota(jnp.int32, (SSD_C, SSD_C), 1)
    mask = (c >= r) if reverse else (c <= r)
    lo = c < SSD_P
    n_chunks = RB // SSD_C
    order = range(n_chunks - 1, -1, -1) if reverse else range(n_chunks)
    n_pairs = SSD_HPG // 2
    for ci in order:
        sl = slice(ci * SSD_C, (ci + 1) * SSD_C)
        x = x_ref[sl, :]
        bm = b_ref[sl, :].astype(BF16)
        cm = c_ref[sl, :].astype(BF16)
        dt = dt_ref[sl, :]
        cum = cum_ref[sl, :]
        cum_t = jnp.transpose(cum)
        cb = lax.dot_general(cm, bm, NT, preferred_element_type=F32)
        h = h_ref[...]
        y_inter = jnp.dot(cm, h.astype(BF16), preferred_element_type=F32)
        e_idx = 0 if reverse else SSD_C - 1
        ys, xdes, decs = [], [], []
        for j in range(n_pairs):
            cols, ls = [], []
            for rr in (2 * j, 2 * j + 1):
                col = jnp.broadcast_to(cum[:, rr:rr + 1], (SSD_C, SSD_C))
                seg = col - cum_t[rr:rr + 1, :]
                ls.append((jnp.exp(jnp.where(mask, seg, -1e30)) * cb).astype(BF16))
                cols.append(col)
            cp = jnp.where(lo, cols[0], cols[1])
            dtp = jnp.where(lo, jnp.broadcast_to(dt[:, 2 * j:2 * j + 1], (SSD_C, SSD_C)),
                            jnp.broadcast_to(dt[:, 2 * j + 1:2 * j + 2], (SSD_C, SSD_C)))
            xp = x[:, j * LANE:(j + 1) * LANE]
            xdt = xp * dtp
            rhs = jnp.concatenate([jnp.where(lo, xdt, 0.0), jnp.where(lo, 0.0, xdt)], axis=0).astype(BF16)
            lhs = jnp.concatenate(ls, axis=1)
            y = jnp.dot(lhs, rhs, preferred_element_type=F32)
            y = y + jnp.exp(cp) * y_inter[:, j * LANE:(j + 1) * LANE]
            if not reverse:
                y = y + dsk_ref[:, j * LANE:(j + 1) * LANE] * xp
            ys.append(y)
            edge = cp[e_idx:e_idx + 1, :]
            xdes.append((xdt * jnp.exp(edge - cp)).astype(BF16))
            decs.append(jnp.exp(edge))
        y_ref[sl, :] = jnp.concatenate(ys, axis=1)
        upd = lax.dot_general(bm, jnp.concatenate(xdes, axis=1), TN, preferred_element_type=F32)
        h_ref[...] = h * jnp.concatenate(decs, axis=1) + upd

    @pl.when(last & is_ctx)
    def _():
        hf_ref[...] = h_ref[...]


def _ssd_scan(lay, xbc, dtg, cumg, h0, dsk, *, reverse, name):
    d = 1 if reverse else 0
    blk = (lambda i: lay.nb - 1 - i) if reverse else (lambda i: i)
    cseq = lambda i: _seq_info(lay, blk(i))[1]
    lseq = lambda i: _seq_info(lay, blk(i))[2]
    gw = SSD_HPG * SSD_P
    return pl.pallas_call(
        functools.partial(_ssd_kernel, lay=lay, reverse=reverse),
        out_shape=(jax.ShapeDtypeStruct((lay.m, D_INNER), F32),
                   jax.ShapeDtypeStruct((lay.n_ctx, SSD_G, SSD_N, gw), F32)),
        grid=(SSD_G, lay.nb),
        in_specs=[
            pl.BlockSpec((RB, gw), lambda g, i: (blk(i), g)),
            pl.BlockSpec((RB, SSD_N), lambda g, i: (blk(i), D_INNER // SSD_N + g)),
            pl.BlockSpec((RB, SSD_N), lambda g, i: (blk(i), (D_INNER + SSD_BC) // SSD_N + g)),
            pl.BlockSpec((None, None, RB, LANE), lambda g, i: (d, g, blk(i), 0)),
            pl.BlockSpec((None, None, RB, LANE), lambda g, i: (d, g, blk(i), 0)),
            pl.BlockSpec((None, None, None, SSD_N, gw), lambda g, i: (lseq(i), d, g, 0, 0)),
            pl.BlockSpec((1, gw), lambda g, i: (0, g)),
        ],
        out_specs=(pl.BlockSpec((RB, gw), lambda g, i: (blk(i), g)),
                   pl.BlockSpec((None, None, SSD_N, gw), lambda g, i: (cseq(i), g, 0, 0))),
        scratch_shapes=[pltpu.VMEM((SSD_N, gw), F32)],
        compiler_params=_cparams(("arbitrary", "arbitrary")),
        name=name,
    )(xbc, xbc, xbc, dtg, cumg, h0, dsk)


def _split3(x):
    hi = x.astype(BF16)
    r1 = x - hi.astype(F32)
    mid = r1.astype(BF16)
    lo = (r1 - mid.astype(F32)).astype(BF16)
    return hi, mid, lo


def _norm_router_kernel(x_ref, gain_ref, scale_ref, shift_ref, wr_ref, h_ref, lg_ref):
    h = _rms(x_ref[...], gain_ref[...]) * (1.0 + scale_ref[...]) + shift_ref[...]
    h_ref[...] = h
    hs = _split3(h)
    ws = _split3(wr_ref[...])
    acc = jnp.zeros(lg_ref.shape, F32)
    for a, b in ((2, 0), (0, 2), (1, 1), (1, 0), (0, 1), (0, 0)):
        acc = acc + jnp.dot(hs[a], ws[b], preferred_element_type=F32)
    lg_ref[...] = acc


def _norm_router(lay, x, gain, mod3, wr):
    mrow = lambda i: _mod_row(lay, i * RB)
    vspec = lambda fn: pl.BlockSpec((None, 1, D), lambda i: (fn(i), 0, 0))
    return pl.pallas_call(
        _norm_router_kernel,
        out_shape=(jax.ShapeDtypeStruct((lay.m, D), F32), jax.ShapeDtypeStruct((lay.m, LANE), F32)),
        grid=(lay.nb,),
        in_specs=[pl.BlockSpec((RB, D), lambda i: (i, 0)), vspec(lambda i: 0),
                  vspec(lambda i: mrow(i) * 6 + 4), vspec(lambda i: mrow(i) * 6 + 3),
                  pl.BlockSpec((D, LANE), lambda i: (0, 0))],
        out_specs=(pl.BlockSpec((RB, D), lambda i: (i, 0)), pl.BlockSpec((RB, LANE), lambda i: (i, 0))),
        compiler_params=_cparams(("arbitrary",)),
        name="norm_router",
    )(x, gain, mod3, mod3, wr)


GATHER_UNROLL = 8


def _gather_rows(src_hbm, idx_ref, n, dst_row, sem, *, wait, inline=False):
    def one(q):
        cp = pltpu.make_async_copy(src_hbm.at[pl.ds(idx_ref[0, q], 1), :], dst_row(q), sem)
        if wait:
            cp.wait()
        else:
            cp.start()

    if inline:
        for q in range(n):
            one(q)
    else:
        def body(q, carry):
            one(q)
            return carry

        lax.fori_loop(0, n, body, 0, unroll=GATHER_UNROLL)


def _prefetched_gather(src_hbm, cur_idx, next_idx, n, dst, sem):
    step = pl.program_id(0)
    slot = step % 2

    @pl.when(step == 0)
    def _():
        _gather_rows(src_hbm, cur_idx, n, dst(0), sem.at[0], wait=False)

    _gather_rows(src_hbm, cur_idx, n, dst(slot), sem.at[slot], wait=True)
    _gather_rows(src_hbm, next_idx, n, dst(1 - slot), sem.at[1 - slot], wait=False, inline=True)
    return slot


def _drain_gather(src_hbm, next_idx, n, dst, sem):
    step = pl.program_id(0)
    slot = step % 2

    @pl.when(step == pl.num_programs(0) - 1)
    def _():
        _gather_rows(src_hbm, next_idx, n, dst(1 - slot), sem.at[1 - slot], wait=True)


def _expert_kernel(be_ref, rt_ref, rtn_ref, h_hbm, w1_ref, w3_ref, w2_ref, o_ref, xbuf, sem):
    dst = lambda s: (lambda q: xbuf.at[s, pl.ds(q, 1), :])
    slot = _prefetched_gather(h_hbm, rt_ref, rtn_ref, MOE_BLOCK, dst, sem)
    x = xbuf[slot].astype(BF16)
    h1 = jnp.dot(x, w1_ref[...], preferred_element_type=F32)
    h3 = jnp.dot(x, w3_ref[...], preferred_element_type=F32)
    hid = (_silu(h1) * h3).astype(BF16)
    o_ref[...] = jnp.dot(hid, w2_ref[...], preferred_element_type=F32)
    _drain_gather(h_hbm, rtn_ref, MOE_BLOCK, dst, sem)


def _experts(h2, row_tok, w1, w3, w2, block_e):
    n_rows = row_tok.shape[0]
    n_blocks = n_rows // MOE_BLOCK
    rt = row_tok.reshape(n_blocks, 1, MOE_BLOCK)
    wspec = lambda a, b: pl.BlockSpec((None, a, b), lambda i, be: (be[i], 0, 0))
    ispec = lambda f: pl.BlockSpec((None, 1, MOE_BLOCK), lambda i, be: (f(i), 0, 0), memory_space=pltpu.SMEM)
    return pl.pallas_call(
        _expert_kernel,
        out_shape=jax.ShapeDtypeStruct((n_rows, D), F32),
        grid_spec=pltpu.PrefetchScalarGridSpec(
            num_scalar_prefetch=1,
            grid=(n_blocks,),
            in_specs=[ispec(lambda i: i), ispec(lambda i: jnp.minimum(i + 1, n_blocks - 1)),
                      pl.BlockSpec(memory_space=pl.ANY),
                      wspec(D, D_EXP), wspec(D, D_EXP), wspec(D_EXP, D)],
            out_specs=pl.BlockSpec((MOE_BLOCK, D), lambda i, be: (i, 0)),
            scratch_shapes=[pltpu.VMEM((2, MOE_BLOCK, D), F32), pltpu.SemaphoreType.DMA((2,))],
        ),
        compiler_params=_cparams(("arbitrary",)),
        name="experts",
    )(block_e, rt, rt, h2, w1, w3, w2)


def _combine_kernel(dc_ref, dn_ref, x_ref, tw_ref, gate_ref, nf_ref, yb_hbm, *rest, lay, final):
    if final:
        op_ref, os_ref, ybuf, sem = rest
    else:
        o_ref, ybuf, sem = rest
    i = pl.program_id(0)
    n_q = RB * TOP_K
    assert TOP_K == 2
    dst = lambda s: (lambda q: ybuf.at[s, q & 1, pl.ds(q >> 1, 1), :])
    slot = _prefetched_gather(yb_hbm, dc_ref, dn_ref, n_q, dst, sem)
    tw = tw_ref[...]
    y = x_ref[...] + gate_ref[...] * (ybuf[slot, 0] * tw[:, 0:1] + ybuf[slot, 1] * tw[:, 1:2])
    if not final:
        o_ref[...] = y
    else:
        y = _rms(y, nf_ref[...])

        @pl.when(i < lay.nb_ctx)
        def _():
            op_ref[...] = y

        @pl.when(i >= lay.nb_ctx)
        def _():
            os_ref[...] = y

    _drain_gather(yb_hbm, dn_ref, n_q, dst, sem)


def _combine(lay, x, yb, dest, tw, mod3, norm_f, *, final):
    mrow = lambda i: _mod_row(lay, i * RB)
    spec = pl.BlockSpec((RB, D), lambda i: (i, 0))
    n_q = RB * TOP_K
    dq = dest.reshape(lay.nb, 1, n_q)
    ispec = lambda f: pl.BlockSpec((None, 1, n_q), lambda i: (f(i), 0, 0), memory_space=pltpu.SMEM)
    if final:
        out_shape = (jax.ShapeDtypeStruct((lay.m_ctx, D), F32), jax.ShapeDtypeStruct((lay.m - lay.m_ctx, D), F32))
        out_specs = (pl.BlockSpec((RB, D), lambda i: (jnp.minimum(i, lay.nb_ctx - 1), 0)),
                     pl.BlockSpec((RB, D), lambda i: (jnp.maximum(i - lay.nb_ctx, 0), 0)))
    else:
        out_shape, out_specs = jax.ShapeDtypeStruct((lay.m, D), F32), spec
    return pl.pallas_call(
        functools.partial(_combine_kernel, lay=lay, final=final),
        out_shape=out_shape,
        grid=(lay.nb,),
        in_specs=[ispec(lambda i: i), ispec(lambda i: jnp.minimum(i + 1, lay.nb - 1)), spec,
                  pl.BlockSpec((RB, TOP_K), lambda i: (i, 0)),
                  pl.BlockSpec((None, 1, D), lambda i: (mrow(i) * 6 + 5, 0, 0)),
                  pl.BlockSpec((1, D), lambda i: (0, 0)),
                  pl.BlockSpec(memory_space=pl.ANY)],
        out_specs=out_specs,
        scratch_shapes=[pltpu.VMEM((2, TOP_K, RB, D), F32), pltpu.SemaphoreType.DMA((2,))],
        compiler_params=_cparams(("arbitrary",)),
        name="final" if final else "combine",
    )(dq, dq, x, tw, mod3, norm_f, yb)


def _route(logits, router_bias, n_tok):
    n_asg = n_tok * TOP_K
    scores = jax.nn.sigmoid(logits.T)
    biased = scores + router_bias.astype(F32)[:, None]
    bg = biased.reshape(N_GRP, EPG, n_tok)
    pair_sums = [bg[:, a] + bg[:, b] for a in range(EPG) for b in range(a + 1, EPG)]
    group_score = functools.reduce(jnp.maximum, pair_sums)
    best_group = jnp.argmax(group_score, axis=0)
    sel = (jnp.arange(N_GRP)[:, None, None] == best_group[None, None, :])
    vb = jnp.sum(jnp.where(sel, bg, 0.0), axis=0)
    vs = jnp.sum(jnp.where(sel, scores.reshape(N_GRP, EPG, n_tok), 0.0), axis=0)
    loc = jnp.arange(EPG)[:, None]
    i1 = jnp.argmax(vb, axis=0)
    i2 = jnp.argmax(jnp.where(loc == i1[None, :], -jnp.inf, vb), axis=0)
    top_loc = jnp.stack([i1, i2], axis=0)
    top_idx = (best_group[None, :] * EPG + top_loc).astype(jnp.int32)
    top_w = jnp.sum(jnp.where(loc[None, :, :] == top_loc[:, None, :], vs[None, :, :], 0.0), axis=1)
    top_w = top_w / jnp.sum(top_w, axis=0, keepdims=True)
    ex = jnp.arange(N_EXP, dtype=jnp.int32)[None, :, None]
    hit = top_idx[:, None, :] == ex
    cb = MOE_BLOCK
    oh = (hit[0] | hit[1]).astype(F32).reshape(N_EXP, n_tok // cb, cb)
    tri = (jnp.arange(cb)[:, None] < jnp.arange(cb)[None, :]).astype(F32)
    within = jnp.einsum('ebs,st->ebt', oh, tri)
    bsum = oh.sum(-1)
    counts = bsum.sum(-1).astype(jnp.int32)
    padded = (counts + MOE_BLOCK - 1) // MOE_BLOCK * MOE_BLOCK
    padded_end = jnp.cumsum(padded)
    padded_start = padded_end - padded
    base = jnp.cumsum(bsum, axis=1) - bsum + padded_start.astype(F32)[:, None]
    pos_e = (within + base[:, :, None]).reshape(N_EXP, n_tok)
    dest = jnp.sum(jnp.where(hit, pos_e[None], 0.0), axis=1).astype(jnp.int32)
    n_blocks = -(-n_asg // MOE_BLOCK) + N_EXP
    n_rows = n_blocks * MOE_BLOCK
    tok = jnp.broadcast_to(jnp.arange(n_tok, dtype=jnp.int32)[None, :], (TOP_K, n_tok))
    row_tok = jnp.zeros((n_rows,), jnp.int32).at[dest.reshape(-1)].set(tok.reshape(-1))
    block_start = jnp.arange(n_blocks, dtype=jnp.int32) * MOE_BLOCK
    block_e = jnp.minimum(jnp.searchsorted(padded_end, block_start, side='right'), N_EXP - 1).astype(jnp.int32)
    return row_tok, top_w.T, block_e, dest.T


def _sincos_2d(rows, cols, dim):
    quarter = dim // 4
    omega = 1.0 / (POS_BASE ** (jnp.arange(quarter, dtype=F32) / quarter))

    def axis_embed(n):
        ang = jnp.arange(n, dtype=F32)[:, None] * omega[None, :]
        return jnp.concatenate([jnp.sin(ang), jnp.cos(ang)], axis=-1)

    er = jnp.broadcast_to(axis_embed(rows)[:, None, :], (rows, cols, dim // 2))
    ec = jnp.broadcast_to(axis_embed(cols)[None, :, :], (rows, cols, dim // 2))
    return jnp.concatenate([er, ec], axis=-1).reshape(rows * cols, dim)


def _rearranged_w_in(w):
    sizes = (GLA_QK, GLA_QK, GLA_V, GLA_V, 2 * GLA_LR, D_INNER, CONV_DIM, 2 * SSD_HEADS, D, D)
    offs = [0]
    for s in sizes:
        offs.append(offs[-1] + s)
    q, k, v, g, dlr, z, xbc, dt, ga, gb = [w[:, offs[i]:offs[i + 1]] for i in range(len(sizes))]
    zpad = jnp.zeros((w.shape[0], LANE - GLA_LR), w.dtype)
    return jnp.concatenate([q, k, v, g, ga, z, xbc, gb, dt, dlr[:, :GLA_LR], zpad, dlr[:, GLA_LR:], zpad,
                            jnp.zeros((w.shape[0], LANE), w.dtype)], axis=1).astype(BF16)


def _pick_tile(lay, want):
    t = want
    while lay.m_ctx % t or lay.t_lat % t:
        t //= 2
    return t


def kernel(x_prompt, x_sample, c, state_gla, state_ssd, c_ctx, w_ada, b_ada, norm1, norm2, w_in, gla_dw2, gla_db, gla_norm, w_gla_b, conv_w, conv_b, a_log, dt_bias, d_skip, ssd_norm, w_ssd_b, w_out, w_router, router_bias, w_e1, w_e3, w_e2, norm_f):
    n_ctx, t_ctx, _ = x_prompt.shape
    n_lat, t_lat, _ = x_sample.shape
    lay = Layout(n_ctx, t_ctx, n_lat, t_lat)
    depth = w_in.shape[0]
    m = lay.m
    assert t_ctx % RB == 0 and t_lat % RB == 0 and t_lat % GRID_W == 0

    pos = _sincos_2d(t_lat // GRID_W, GRID_W, D)
    x = _embed(lay, x_prompt.reshape(lay.m_ctx, D), x_sample.reshape(n_lat * t_lat, D), pos)

    n_mod = 16
    cond = jnp.concatenate([c_ctx[None, :], c, jnp.zeros((n_mod - 1 - n_lat, D), F32)], axis=0)
    wr = jnp.pad(w_router, ((0, 0), (0, LANE - N_EXP)))

    tm_in = _pick_tile(lay, 1024)
    tm_a = _pick_tile(lay, 512)
    mrow = lambda tm: (lambda i: _mod_row(lay, i * tm))
    new_gla, new_ssd = [], []
    y_prompt = y_sample = None
    for l in range(depth):
        mod = _fused_mm("ada", n_mod, D, 6 * D, tm=n_mod, tn=1024, rows=[(cond, D, 0)], vecs=[],
                        w=w_ada[l].astype(BF16), evecs=[(b_ada[l].reshape(1, 1, 6 * D), lambda i: 0)],
                        prologue=_pro_silu, epilogue=_epi_bias, out_dtype=F32, rc=n_mod)
        mod3 = mod.reshape(n_mod * 6, 1, D)
        n1 = norm1[l].reshape(1, 1, D)
        mr = mrow(tm_in)
        proj = _fused_mm("in_proj", m, D, NP, tm=tm_in, tn=512, rows=[(x, D, 0)],
                         vecs=[(n1, D, lambda i: 0), (mod3, D, lambda i: mr(i) * 6 + 1),
                               (mod3, D, lambda i: mr(i) * 6 + 0)],
                         w=_rearranged_w_in(w_in[l]), prologue=_pro_norm_mod, epilogue=None, out_dtype=F32)

        dw2p = jnp.pad(gla_dw2[l], ((0, 0), (0, LANE - GLA_LR), (0, 0))).astype(BF16)
        db = gla_db[l].reshape(2, 1, GLA_QK)
        o_f, sg_f = _gla_scan(lay, proj, dw2p, db, state_gla, l, reverse=False, name="gla_fwd")
        o_b, sg_b = _gla_scan(lay, proj, dw2p, db, state_gla, l, reverse=True, name="gla_bwd")
        new_gla.append(jnp.stack([sg_f, sg_b], axis=1))
        m_a = _kmm("gla_out", m, GLA_V, D, tm=tm_a, tk=GLA_DV,
                   rows=[(o_f, 0), (o_b, 0), (proj, C_G // GLA_DV)], vecs=[gla_norm[l].reshape(1, GLA_V)],
                   w=w_gla_b[l].astype(BF16), erows=[(proj, C_GA // D)],
                   prologue=_pro_gla_out, epilogue=_epi_gate_a, out_dtype=F32)

        xbc = _conv(lay, proj, conv_w[l], conv_b[l].reshape(1, CONV_DIM))
        dtg, cumg = _dtprep(lay, proj, dt_bias[l].reshape(1, LANE), a_log[l].reshape(1, LANE))
        h0 = state_ssd[:, l].reshape(n_lat, 2, SSD_G, SSD_HPG, SSD_N, SSD_P)
        h0 = h0.transpose(0, 1, 2, 4, 3, 5).reshape(n_lat, 2, SSD_G, SSD_N, SSD_HPG * SSD_P)
        dsk = jnp.repeat(d_skip[l], SSD_P).reshape(1, D_INNER)
        y_f, hs_f = _ssd_scan(lay, xbc, dtg, cumg, h0, dsk, reverse=False, name="ssd_fwd")
        y_b, hs_b = _ssd_scan(lay, xbc, dtg, cumg, h0, dsk, reverse=True, name="ssd_bwd")
        hs = jnp.stack([hs_f, hs_b], axis=1).reshape(n_ctx, 2, SSD_G, SSD_N, SSD_HPG, SSD_P)
        new_ssd.append(hs.transpose(0, 1, 2, 4, 3, 5).reshape(n_ctx, 2, SSD_HEADS, SSD_N, SSD_P))
        merged = _kmm("ssd_out", m, D_INNER, D, tm=tm_a, tk=512,
                      rows=[(y_f, 0), (y_b, 0), (proj, C_Z // 512)], vecs=[ssd_norm[l].reshape(1, D_INNER)],
                      w=w_ssd_b[l].astype(BF16), erows=[(m_a, 0), (proj, C_GB // D)],
                      prologue=_pro_ssd_out, epilogue=_epi_gate_b, out_dtype=BF16)

        x = _fused_mm("mix_out", m, D, D, tm=tm_in, tn=512, rows=[(merged, D, 0)], vecs=[],
                      w=w_out[l].astype(BF16), erows=[(x, 0)], evecs=[(mod3, lambda i: mr(i) * 6 + 2)],
                      prologue=None, epilogue=_epi_resid, out_dtype=F32)

        h2, logits = _norm_router(lay, x, norm2[l].reshape(1, 1, D), mod3, wr)
        row_tok, top_w, block_e, pos_tk = _route(logits[:, :N_EXP], router_bias, m)
        yb = _experts(h2, row_tok, w_e1[l].astype(BF16), w_e3[l].astype(BF16), w_e2[l].astype(BF16), block_e)
        if l + 1 < depth:
            x = _combine(lay, x, yb, pos_tk, top_w, mod3, norm_f.reshape(1, D), final=False)
        else:
            y_prompt, y_sample = _combine(lay, x, yb, pos_tk, top_w, mod3, norm_f.reshape(1, D), final=True)

    return (y_prompt.reshape(n_ctx, t_ctx, D), y_sample.reshape(n_lat, t_lat, D),
            jnp.stack(new_gla, axis=1), jnp.stack(new_ssd, axis=1))
```

```python
import functools
import math
from typing import NamedTuple

import jax
import jax.numpy as jnp
from jax import lax
from jax.experimental import pallas as pl
from jax.experimental.pallas import tpu as pltpu

F32 = jnp.float32
BF16 = jnp.bfloat16

D = 2048
GRID_W = 64
POS_BASE = 10000.0
EPS = 1e-6
GLA_H = 4
GLA_DK = D // 2 // GLA_H
GLA_DV = D // GLA_H
GLA_QK = GLA_H * GLA_DK
GLA_V = GLA_H * GLA_DV
GLA_LR = 16
GLA_TAU = 16.0
GLA_C = 64
D_INNER = 2 * D
SSD_P = 64
SSD_HEADS = D_INNER // SSD_P
SSD_N = 128
SSD_G = 8
SSD_HPG = SSD_HEADS // SSD_G
SSD_BC = SSD_G * SSD_N
CONV_DIM = D_INNER + 2 * SSD_BC
D_CONV = 5
SSD_C = 128
N_EXP = 16
N_GRP = 4
EPG = N_EXP // N_GRP
TOP_K = 2
D_EXP = D // 2
MOE_BLOCK = 256

LANE = 128
SUBLANE = 8
RB = 256
VMEM_LIMIT = 56 * 1024 * 1024

C_Q = 0
C_K = C_Q + GLA_QK
C_V = C_K + GLA_QK
C_G = C_V + GLA_V
C_GA = C_G + GLA_V
C_Z = C_GA + D
C_XBC = C_Z + D_INNER
C_B = C_XBC + D_INNER
C_CM = C_B + SSD_BC
C_GB = C_XBC + CONV_DIM
C_DT = C_GB + D
C_DLR = C_DT + 2 * SSD_HEADS
NP = C_DLR + 3 * LANE


class Layout(NamedTuple):
    n_ctx: int
    t_ctx: int
    n_lat: int
    t_lat: int

    @property
    def m_ctx(self):
        return self.n_ctx * self.t_ctx

    @property
    def m(self):
        return self.m_ctx + self.n_lat * self.t_lat

    @property
    def nb(self):
        return self.m // RB

    @property
    def nb_ctx(self):
        return self.m_ctx // RB

    @property
    def bps_ctx(self):
        return self.t_ctx // RB

    @property
    def bps_lat(self):
        return self.t_lat // RB


def _seq_info(lay, blk):
    is_ctx = blk < lay.nb_ctx
    lb = jnp.maximum(blk - lay.nb_ctx, 0)
    cb = jnp.minimum(blk, lay.nb_ctx - 1)
    pos = jnp.where(is_ctx, cb % lay.bps_ctx, lb % lay.bps_lat)
    bps = jnp.where(is_ctx, lay.bps_ctx, lay.bps_lat)
    return is_ctx, cb // lay.bps_ctx, jnp.minimum(lb // lay.bps_lat, lay.n_lat - 1), pos, bps


def _mod_row(lay, row_start):
    return jnp.where(row_start < lay.m_ctx, 0, 1 + jnp.maximum(row_start - lay.m_ctx, 0) // lay.t_lat)


def _cparams(sem):
    return pltpu.CompilerParams(dimension_semantics=sem, vmem_limit_bytes=VMEM_LIMIT)


def _silu(x):
    return x * jax.nn.sigmoid(x)


def _softplus(x):
    return jnp.maximum(x, 0.0) + jnp.log1p(jnp.exp(-jnp.abs(x)))


def _cumsum_rows(x, n, reverse):
    rows = lax.broadcasted_iota(jnp.int32, x.shape, 0)
    s = 1
    while s < n:
        if reverse:
            x = x + jnp.where(rows < n - s, pltpu.roll(x, n - s, 0), 0.0)
        else:
            x = x + jnp.where(rows >= s, pltpu.roll(x, s, 0), 0.0)
        s *= 2
    return x


NT = (((1,), (1,)), ((), ()))
TN = (((0,), (0,)), ((), ()))


def _embed_kernel(xp_ref, xs_ref, pos_ref, o_ref, *, lay):
    rb = pl.program_id(0)

    @pl.when(rb < lay.nb_ctx)
    def _():
        o_ref[...] = xp_ref[...]

    @pl.when(rb >= lay.nb_ctx)
    def _():
        o_ref[...] = xs_ref[...] + pos_ref[...]


def _embed(lay, xp, xs, pos):
    spec = lambda f: pl.BlockSpec((RB, D), f)
    return pl.pallas_call(
        functools.partial(_embed_kernel, lay=lay),
        out_shape=jax.ShapeDtypeStruct((lay.m, D), F32),
        grid=(lay.nb,),
        in_specs=[spec(lambda i: (jnp.minimum(i, lay.nb_ctx - 1), 0)),
                  spec(lambda i: (jnp.maximum(i - lay.nb_ctx, 0), 0)),
                  spec(lambda i: (jnp.maximum(i - lay.nb_ctx, 0) % lay.bps_lat, 0))],
        out_specs=spec(lambda i: (i, 0)),
        compiler_params=_cparams(("arbitrary",)),
        name="embed",
    )(xp, xs, pos)


def _fused_mm_kernel(*refs, n_rows, n_vecs, n_erows, n_evecs, prologue, epilogue, tm, rc):
    rows = refs[:n_rows]
    vecs = refs[n_rows:n_rows + n_vecs]
    w_ref = refs[n_rows + n_vecs]
    p = n_rows + n_vecs + 1
    erows = refs[p:p + n_erows]
    evecs = refs[p + n_erows:p + n_erows + n_evecs]
    o_ref = refs[p + n_erows + n_evecs]

    if prologue is None:
        lhs = rows[0][...]
    else:
        h_ref = refs[p + n_erows + n_evecs + 1]

        @pl.when(pl.program_id(1) == 0)
        def _():
            vs = [v[...] for v in vecs]

            def chunk(c, carry):
                sl = pl.ds(pl.multiple_of(c * rc, rc), rc)
                h_ref[sl, :] = prologue([r[sl, :] for r in rows], vs).astype(BF16)
                return carry

            lax.fori_loop(0, tm // rc, chunk, 0)

        lhs = h_ref[...]
    acc = jnp.dot(lhs, w_ref[...], preferred_element_type=F32)
    if epilogue is not None:
        acc = epilogue(acc, [e[...] for e in erows], [v[...] for v in evecs])
    o_ref[...] = acc.astype(o_ref.dtype)


def _fused_mm(name, m, k, n, *, tm, tn, rows, vecs, w, w_col0=0, erows=(), evecs=(), prologue, epilogue,
              out_dtype, rc=32):
    in_specs, args = [], []
    for arr, width, cb in rows:
        in_specs.append(pl.BlockSpec((tm, width), lambda i, j, cb=cb: (i, cb)))
        args.append(arr)
    for arr, width, fn in vecs:
        in_specs.append(pl.BlockSpec((None, 1, width), lambda i, j, fn=fn: (fn(i), 0, 0)))
        args.append(arr)
    in_specs.append(pl.BlockSpec((k, tn), lambda i, j: (0, w_col0 + j)))
    args.append(w)
    for arr, cb0 in erows:
        in_specs.append(pl.BlockSpec((tm, tn), lambda i, j, cb0=cb0: (i, cb0 + j)))
        args.append(arr)
    for arr, fn in evecs:
        in_specs.append(pl.BlockSpec((None, 1, tn), lambda i, j, fn=fn: (fn(i), 0, j)))
        args.append(arr)
    scratch = [] if prologue is None else [pltpu.VMEM((tm, k), BF16)]
    kern = functools.partial(_fused_mm_kernel, n_rows=len(rows), n_vecs=len(vecs), n_erows=len(erows),
                             n_evecs=len(evecs), prologue=prologue, epilogue=epilogue, tm=tm, rc=rc)
    return pl.pallas_call(
        kern,
        out_shape=jax.ShapeDtypeStruct((m, n), out_dtype),
        grid=(m // tm, n // tn),
        in_specs=in_specs,
        out_specs=pl.BlockSpec((tm, tn), lambda i, j: (i, j)),
        scratch_shapes=scratch,
        compiler_params=_cparams(("arbitrary", "arbitrary")),
        name=name,
    )(*args)


def _rms(x, gain):
    return x * lax.rsqrt(jnp.mean(x * x, axis=-1, keepdims=True) + EPS) * gain


def _pro_norm_mod(rows, vecs):
    (x,), (gain, scale, shift) = rows, vecs
    return _rms(x, gain) * (1.0 + scale) + shift


def _pro_silu(rows, vecs):
    return _silu(rows[0])


def _epi_bias(acc, erows, evecs):
    return acc + evecs[0]


def _epi_resid(acc, erows, evecs):
    return erows[0] + evecs[0] * acc


def _kmm_kernel(*refs, n_rows, n_vecs, n_erows, prologue, epilogue, tm, rc, nk):
    rows = refs[:n_rows]
    vecs = refs[n_rows:n_rows + n_vecs]
    w_ref = refs[n_rows + n_vecs]
    p = n_rows + n_vecs + 1
    erows = refs[p:p + n_erows]
    o_ref, acc_ref, lhs_ref, ss_ref = refs[p + n_erows:p + n_erows + 4]
    k = pl.program_id(1)

    @pl.when(k == 0)
    def _():
        acc_ref[...] = jnp.zeros_like(acc_ref)
        ss_ref[...] = jnp.zeros_like(ss_ref)

    vs = [v[...] for v in vecs]

    def chunk(c, carry):
        sl = pl.ds(pl.multiple_of(c * rc, rc), rc)
        lhs, ss = prologue([r[sl, :] for r in rows], vs)
        lhs_ref[sl, :] = lhs.astype(BF16)
        if ss is not None:
            ss_ref[sl, :] += ss
        return carry

    lax.fori_loop(0, tm // rc, chunk, 0)
    acc_ref[...] += jnp.dot(lhs_ref[...], w_ref[...], preferred_element_type=F32)

    @pl.when(k == nk - 1)
    def _():
        o_ref[...] = epilogue(acc_ref[...], ss_ref[...], [e[...] for e in erows]).astype(o_ref.dtype)


def _kmm(name, m, k, n, *, tm, tk, rows, vecs, w, erows, prologue, epilogue, out_dtype, rc=64):
    in_specs, args = [], []
    for arr, cb0 in rows:
        in_specs.append(pl.BlockSpec((tm, tk), lambda i, kk, cb0=cb0: (i, cb0 + kk)))
        args.append(arr)
    for arr in vecs:
        in_specs.append(pl.BlockSpec((1, tk), lambda i, kk: (0, kk)))
        args.append(arr)
    in_specs.append(pl.BlockSpec((tk, n), lambda i, kk: (kk, 0)))
    args.append(w)
    for arr, cb in erows:
        in_specs.append(pl.BlockSpec((tm, n), lambda i, kk, cb=cb: (i, cb)))
        args.append(arr)
    kern = functools.partial(_kmm_kernel, n_rows=len(rows), n_vecs=len(vecs), n_erows=len(erows),
                             prologue=prologue, epilogue=epilogue, tm=tm, rc=rc, nk=k // tk)
    return pl.pallas_call(
        kern,
        out_shape=jax.ShapeDtypeStruct((m, n), out_dtype),
        grid=(m // tm, k // tk),
        in_specs=in_specs,
        out_specs=pl.BlockSpec((tm, n), lambda i, kk: (i, 0)),
        scratch_shapes=[pltpu.VMEM((tm, n), F32), pltpu.VMEM((tm, tk), BF16), pltpu.VMEM((tm, 1), F32)],
        compiler_params=_cparams(("arbitrary", "arbitrary")),
        name=name,
    )(*args)


def _pro_gla_out(rows, vecs):
    (of, ob, g), (gain,) = rows, vecs
    return _rms(of + ob, gain) * _silu(g), None


def _epi_gate_a(acc, ss, erows):
    return jax.nn.sigmoid(erows[0]) * acc


def _pro_ssd_out(rows, vecs):
    (yf, yb, z), (gain,) = rows, vecs
    u = (yf + yb) * _silu(z)
    return u * gain, jnp.sum(u * u, axis=-1, keepdims=True)


def _epi_gate_b(acc, ss, erows):
    rstd = lax.rsqrt(ss * (1.0 / D_INNER) + EPS)
    return erows[0] + jax.nn.sigmoid(erows[1]) * (acc * rstd)


def _gla_kernel(q_ref, k_ref, v_ref, dlr_ref, dw_ref, db_ref, s0_ref, o_ref, sf_ref, s_ref, *, lay, reverse):
    i = pl.program_id(1)
    blk = (lay.nb - 1 - i) if reverse else i
    is_ctx, _, _, pos, bps = _seq_info(lay, blk)
    first = (pos == bps - 1) if reverse else (pos == 0)
    last = (pos == 0) if reverse else (pos == bps - 1)

    @pl.when(first & is_ctx)
    def _():
        s_ref[...] = jnp.zeros_like(s_ref)

    @pl.when(first & jnp.logical_not(is_ctx))
    def _():
        s_ref[...] = s0_ref[...]

    x = jnp.dot(dlr_ref[...].astype(BF16), dw_ref[...], preferred_element_type=F32) + db_ref[...]
    log_a = (jnp.minimum(x, 0.0) - jnp.log1p(jnp.exp(-jnp.abs(x)))) * (1.0 / GLA_TAU)

    r = lax.broadcasted_iota(jnp.int32, (GLA_C, GLA_C), 0)
    c = lax.broadcasted_iota(jnp.int32, (GLA_C, GLA_C), 1)
    mask = (c >= r) if reverse else (c <= r)
    r2 = lax.broadcasted_iota(jnp.int32, (2 * GLA_C, GLA_C), 0)
    c2i = lax.broadcasted_iota(jnp.int32, (2 * GLA_C, GLA_C), 1)
    mask_xa = ((c2i >= r2) if reverse else (c2i <= r2)) | (r2 >= GLA_C)
    c2 = 2 * GLA_C
    for pi in (range(RB // c2 - 1, -1, -1) if reverse else range(RB // c2)):
        lo_sl = slice(pi * c2, pi * c2 + GLA_C)
        hi_sl = slice(pi * c2 + GLA_C, (pi + 1) * c2)
        sl_a, sl_b = (hi_sl, lo_sl) if reverse else (lo_sl, hi_sl)
        end_row = (lambda b: b[0:1, :]) if reverse else (lambda b: b[GLA_C - 1:GLA_C, :])
        b_a = _cumsum_rows(log_a[sl_a], GLA_C, reverse)
        b_b = _cumsum_rows(log_a[sl_b], GLA_C, reverse)
        end_a, end_b = end_row(b_a), end_row(b_b)
        q_a = q_ref[sl_a, :] * (GLA_DK ** -0.5)
        q_b = q_ref[sl_b, :] * (GLA_DK ** -0.5)
        k_a, k_b = k_ref[sl_a, :], k_ref[sl_b, :]
        v_a, v_b = v_ref[sl_a, :].astype(BF16), v_ref[sl_b, :].astype(BF16)
        qe_a = (q_a * jnp.exp(b_a)).astype(BF16)
        qe_b = (q_b * jnp.exp(b_b)).astype(BF16)
        qs_b = (q_b * jnp.exp(b_b) * jnp.exp(end_a)).astype(BF16)
        ke_a = (k_a * jnp.exp(-b_a)).astype(BF16)
        ke_b = (k_b * jnp.exp(-b_b)).astype(BF16)
        qq = jnp.concatenate([qe_a, qs_b], axis=0)
        att_xa = jnp.where(mask_xa, lax.dot_general(qq, ke_a, NT, preferred_element_type=F32), 0.0)
        att_bb = jnp.where(mask, lax.dot_general(qe_b, ke_b, NT, preferred_element_type=F32), 0.0)
        s = s_ref[...]
        o_ab = (jnp.dot(att_xa.astype(BF16), v_a, preferred_element_type=F32)
                + jnp.dot(qq, s.astype(BF16), preferred_element_type=F32))
        o_ref[sl_a, :] = o_ab[:GLA_C]
        o_ref[sl_b, :] = o_ab[GLA_C:] + jnp.dot(att_bb.astype(BF16), v_b, preferred_element_type=F32)
        kd_a = (k_a * jnp.exp(end_a - b_a) * jnp.exp(end_b)).astype(BF16)
        kd_b = (k_b * jnp.exp(end_b - b_b)).astype(BF16)
        e_col = jnp.exp(jnp.transpose(jnp.broadcast_to(end_a + end_b, (LANE, GLA_DK))))
        upd = lax.dot_general(jnp.concatenate([kd_a, kd_b], axis=0), jnp.concatenate([v_a, v_b], axis=0), TN,
                              preferred_element_type=F32)
        s_ref[...] = s * jnp.concatenate([e_col] * (GLA_DV // LANE), axis=1) + upd

    @pl.when(last & is_ctx)
    def _():
        sf_ref[...] = s_ref[...]


def _gla_scan(lay, proj, dw2p, db, s0, layer, *, reverse, name):
    d = 1 if reverse else 0
    blk = (lambda i: lay.nb - 1 - i) if reverse else (lambda i: i)
    cseq = lambda i: _seq_info(lay, blk(i))[1]
    lseq = lambda i: _seq_info(lay, blk(i))[2]
    return pl.pallas_call(
        functools.partial(_gla_kernel, lay=lay, reverse=reverse),
        out_shape=(jax.ShapeDtypeStruct((lay.m, GLA_V), F32),
                   jax.ShapeDtypeStruct((lay.n_ctx, GLA_H, GLA_DK, GLA_DV), F32)),
        grid=(GLA_H, lay.nb),
        in_specs=[
            pl.BlockSpec((RB, GLA_DK), lambda h, i: (blk(i), C_Q // GLA_DK + h)),
            pl.BlockSpec((RB, GLA_DK), lambda h, i: (blk(i), C_K // GLA_DK + h)),
            pl.BlockSpec((RB, GLA_DV), lambda h, i: (blk(i), C_V // GLA_DV + h)),
            pl.BlockSpec((RB, LANE), lambda h, i: (blk(i), C_DLR // LANE + d)),
            pl.BlockSpec((None, LANE, GLA_DK), lambda h, i: (d, 0, h)),
            pl.BlockSpec((None, 1, GLA_DK), lambda h, i: (d, 0, h)),
            pl.BlockSpec((None, None, None, None, GLA_DK, GLA_DV), lambda h, i: (lseq(i), layer, d, h, 0, 0)),
        ],
        out_specs=(pl.BlockSpec((RB, GLA_DV), lambda h, i: (blk(i), h)),
                   pl.BlockSpec((None, None, GLA_DK, GLA_DV), lambda h, i: (cseq(i), h, 0, 0))),
        scratch_shapes=[pltpu.VMEM((GLA_DK, GLA_DV), F32)],
        compiler_params=_cparams(("arbitrary", "arbitrary")),
        name=name,
    )(proj, proj, proj, proj, dw2p, db, s0)


CONV_TC = 2048
CONV_SC = 512


def _conv_kernel(prev_ref, cur_ref, next_ref, w_ref, b_ref, o_ref, *, lay):
    rb = pl.program_id(0)
    _, _, _, pos, bps = _seq_info(lay, rb)
    n = RB + 2 * SUBLANE
    for s in range(CONV_TC // CONV_SC):
        cs = slice(s * CONV_SC, (s + 1) * CONV_SC)
        prev = jnp.where(pos > 0, prev_ref[:, cs], 0.0)
        nxt = jnp.where(pos < bps - 1, next_ref[:, cs], 0.0)
        ext = jnp.concatenate([prev, cur_ref[:, cs], nxt], axis=0)
        w = w_ref[:, cs]
        acc = jnp.broadcast_to(b_ref[:, cs], (RB, CONV_SC))
        for j in range(D_CONV):
            off = j - D_CONV // 2
            sh = ext if off == 0 else pltpu.roll(ext, (-off) % n, 0)
            acc = acc + w[j:j + 1, :] * sh[SUBLANE:SUBLANE + RB, :]
        o_ref[:, cs] = _silu(acc)


def _conv(lay, proj, cw, cb):
    c0 = C_XBC // CONV_TC
    rpb = RB // SUBLANE
    return pl.pallas_call(
        functools.partial(_conv_kernel, lay=lay),
        out_shape=jax.ShapeDtypeStruct((lay.m, CONV_DIM), F32),
        grid=(lay.nb, CONV_DIM // CONV_TC),
        in_specs=[
            pl.BlockSpec((SUBLANE, CONV_TC), lambda i, j: (jnp.maximum(i * rpb - 1, 0), c0 + j)),
            pl.BlockSpec((RB, CONV_TC), lambda i, j: (i, c0 + j)),
            pl.BlockSpec((SUBLANE, CONV_TC), lambda i, j: (jnp.minimum((i + 1) * rpb, lay.m // SUBLANE - 1), c0 + j)),
            pl.BlockSpec((D_CONV, CONV_TC), lambda i, j: (0, j)),
            pl.BlockSpec((1, CONV_TC), lambda i, j: (0, j)),
        ],
        out_specs=pl.BlockSpec((RB, CONV_TC), lambda i, j: (i, j)),
        compiler_params=_cparams(("arbitrary", "arbitrary")),
        name="conv",
    )(proj, proj, proj, cw, cb)


def _dtprep_kernel(raw_ref, bias_ref, alog_ref, dt_ref, cum_ref):
    dt = _softplus(raw_ref[...] + bias_ref[...])
    dta = dt * (-jnp.exp(alog_ref[...]))
    lane = lax.broadcasted_iota(jnp.int32, (SSD_C, LANE), 1)
    cums = []
    for ci in range(RB // SSD_C):
        t = dta[ci * SSD_C:(ci + 1) * SSD_C]
        cums.append(jnp.where(lane < SSD_HEADS, _cumsum_rows(t, SSD_C, False), _cumsum_rows(t, SSD_C, True)))
    cum = jnp.concatenate(cums, axis=0)
    for d in range(2):
        for g in range(SSD_G):
            base = d * SSD_HEADS + g * SSD_HPG
            sh = (LANE - base) % LANE
            dt_ref[d, g] = dt if sh == 0 else pltpu.roll(dt, sh, 1)
            cum_ref[d, g] = cum if sh == 0 else pltpu.roll(cum, sh, 1)


def _dtprep(lay, proj, dt_bias, a_log):
    out = jax.ShapeDtypeStruct((2, SSD_G, lay.m, LANE), F32)
    ospec = pl.BlockSpec((2, SSD_G, RB, LANE), lambda i: (0, 0, i, 0))
    vspec = pl.BlockSpec((1, LANE), lambda i: (0, 0))
    return pl.pallas_call(
        _dtprep_kernel,
        out_shape=(out, out),
        grid=(lay.nb,),
        in_specs=[pl.BlockSpec((RB, LANE), lambda i: (i, C_DT // LANE)), vspec, vspec],
        out_specs=(ospec, ospec),
        compiler_params=_cparams(("arbitrary",)),
        name="dtprep",
    )(proj, dt_bias, a_log)


def _ssd_kernel(x_ref, b_ref, c_ref, dt_ref, cum_ref, h0_ref, dsk_ref, y_ref, hf_ref, h_ref, *, lay, reverse):
    i = pl.program_id(1)
    blk = (lay.nb - 1 - i) if reverse else i
    is_ctx, _, _, pos, bps = _seq_info(lay, blk)
    first = (pos == bps - 1) if reverse else (pos == 0)
    last = (pos == 0) if reverse else (pos == bps - 1)

    @pl.when(first & is_ctx)
    def _():
        h_ref[...] = jnp.zeros_like(h_ref)

    @pl.when(first & jnp.logical_not(is_ctx))
    def _():
        h_ref[...] = h0_ref[...]

    r = lax.broadcasted_iota(jnp.int32, (SSD_C, SSD_C), 0)
    c = lax.broadcasted_iota(jnp.int32, (SSD_C, SSD_C), 1)
    mask = (c >= r) if reverse else (c <= r)
    lo = c < SSD_P
    gw = SSD_HPG * SSD_P
    spread = jnp.where(lax.broadcasted_iota(jnp.int32, (LANE, gw), 0)
                       == (lax.broadcasted_iota(jnp.int32, (LANE, gw), 1) >> int(math.log2(SSD_P))),
                       1.0, 0.0).astype(BF16)
    n_chunks = RB // SSD_C
    order = range(n_chunks - 1, -1, -1) if reverse else range(n_chunks)
    n_pairs = SSD_HPG // 2
    for ci in order:
        sl = slice(ci * SSD_C, (ci + 1) * SSD_C)
        x = x_ref[sl, :]
        bm = b_ref[sl, :].astype(BF16)
        cm = c_ref[sl, :].astype(BF16)
        dt_x = sum(jnp.dot(piece, spread, preferred_element_type=F32) for piece in _split3(dt_ref[sl, :]))
        cum = cum_ref[sl, :]
        cum_t = jnp.transpose(cum)
        cb = lax.dot_general(cm, bm, NT, preferred_element_type=F32)
        h = h_ref[...]
        y_inter = jnp.dot(cm, h.astype(BF16), preferred_element_type=F32)
        e_idx = 0 if reverse else SSD_C - 1
        ys, xdes, decs = [], [], []
        for j in range(n_pairs):
            cols, ls = [], []
            for rr in (2 * j, 2 * j + 1):
                col = jnp.broadcast_to(cum[:, rr:rr + 1], (SSD_C, SSD_C))
                seg = col - cum_t[rr:rr + 1, :]
                ls.append((jnp.exp(jnp.where(mask, seg, -1e30)) * cb).astype(BF16))
                cols.append(col)
            cp = jnp.where(lo, cols[0], cols[1])
            dtp = dt_x[:, j * LANE:(j + 1) * LANE]
            xp = x[:, j * LANE:(j + 1) * LANE]
            xdt = xp * dtp
            rhs = jnp.concatenate([jnp.where(lo, xdt, 0.0), jnp.where(lo, 0.0, xdt)], axis=0).astype(BF16)
            lhs = jnp.concatenate(ls, axis=1)
            y = jnp.dot(lhs, rhs, preferred_element_type=F32)
            y = y + jnp.exp(cp) * y_inter[:, j * LANE:(j + 1) * LANE]
            if not reverse:
                y = y + dsk_ref[:, j * LANE:(j + 1) * LANE] * xp
            ys.append(y)
            edge = cp[e_idx:e_idx + 1, :]
            xdes.append((xdt * jnp.exp(edge - cp)).astype(BF16))
            decs.append(jnp.exp(edge))
        y_ref[sl, :] = jnp.concatenate(ys, axis=1)
        upd = lax.dot_general(bm, jnp.concatenate(xdes, axis=1), TN, preferred_element_type=F32)
        h_ref[...] = h * jnp.concatenate(decs, axis=1) + upd

    @pl.when(last & is_ctx)
    def _():
        hf_ref[...] = h_ref[...]


def _ssd_scan(lay, xbc, dtg, cumg, h0, dsk, *, reverse, name):
    d = 1 if reverse else 0
    blk = (lambda i: lay.nb - 1 - i) if reverse else (lambda i: i)
    cseq = lambda i: _seq_info(lay, blk(i))[1]
    lseq = lambda i: _seq_info(lay, blk(i))[2]
    gw = SSD_HPG * SSD_P
    return pl.pallas_call(
        functools.partial(_ssd_kernel, lay=lay, reverse=reverse),
        out_shape=(jax.ShapeDtypeStruct((lay.m, D_INNER), F32),
                   jax.ShapeDtypeStruct((lay.n_ctx, SSD_G, SSD_N, gw), F32)),
        grid=(SSD_G, lay.nb),
        in_specs=[
            pl.BlockSpec((RB, gw), lambda g, i: (blk(i), g)),
            pl.BlockSpec((RB, SSD_N), lambda g, i: (blk(i), D_INNER // SSD_N + g)),
            pl.BlockSpec((RB, SSD_N), lambda g, i: (blk(i), (D_INNER + SSD_BC) // SSD_N + g)),
            pl.BlockSpec((None, None, RB, LANE), lambda g, i: (d, g, blk(i), 0)),
            pl.BlockSpec((None, None, RB, LANE), lambda g, i: (d, g, blk(i), 0)),
            pl.BlockSpec((None, None, None, SSD_N, gw), lambda g, i: (lseq(i), d, g, 0, 0)),
            pl.BlockSpec((1, gw), lambda g, i: (0, g)),
        ],
        out_specs=(pl.BlockSpec((RB, gw), lambda g, i: (blk(i), g)),
                   pl.BlockSpec((None, None, SSD_N, gw), lambda g, i: (cseq(i), g, 0, 0))),
        scratch_shapes=[pltpu.VMEM((SSD_N, gw), F32)],
        compiler_params=_cparams(("arbitrary", "arbitrary")),
        name=name,
    )(xbc, xbc, xbc, dtg, cumg, h0, dsk)


def _split3(x):
    hi = x.astype(BF16)
    r1 = x - hi.astype(F32)
    mid = r1.astype(BF16)
    lo = (r1 - mid.astype(F32)).astype(BF16)
    return hi, mid, lo


def _norm_router_kernel(x_ref, gain_ref, scale_ref, shift_ref, wr_ref, h_ref, lg_ref):
    h = _rms(x_ref[...], gain_ref[...]) * (1.0 + scale_ref[...]) + shift_ref[...]
    h_ref[...] = h
    lg_ref[...] = jnp.dot(h.astype(BF16), wr_ref[...].astype(BF16), preferred_element_type=F32)


def _norm_router(lay, x, gain, mod3, wr):
    mrow = lambda i: _mod_row(lay, i * RB)
    vspec = lambda fn: pl.BlockSpec((None, 1, D), lambda i: (fn(i), 0, 0))
    return pl.pallas_call(
        _norm_router_kernel,
        out_shape=(jax.ShapeDtypeStruct((lay.m, D), F32), jax.ShapeDtypeStruct((lay.m, LANE), F32)),
        grid=(lay.nb,),
        in_specs=[pl.BlockSpec((RB, D), lambda i: (i, 0)), vspec(lambda i: 0),
                  vspec(lambda i: mrow(i) * 6 + 4), vspec(lambda i: mrow(i) * 6 + 3),
                  pl.BlockSpec((D, LANE), lambda i: (0, 0))],
        out_specs=(pl.BlockSpec((RB, D), lambda i: (i, 0)), pl.BlockSpec((RB, LANE), lambda i: (i, 0))),
        compiler_params=_cparams(("arbitrary",)),
        name="norm_router",
    )(x, gain, mod3, mod3, wr)


GATHER_UNROLL = 8


def _gather_rows(src_hbm, idx_ref, n, dst_row, sem, *, wait, inline=False):
    def one(q):
        cp = pltpu.make_async_copy(src_hbm.at[pl.ds(idx_ref[0, q], 1), :], dst_row(q), sem)
        if wait:
            cp.wait()
        else:
            cp.start()

    if inline:
        for q in range(n):
            one(q)
    else:
        def body(q, carry):
            one(q)
            return carry

        lax.fori_loop(0, n, body, 0, unroll=GATHER_UNROLL)


def _prefetched_gather(src_hbm, cur_idx, next_idx, n, dst, sem):
    step = pl.program_id(0)
    slot = step % 2

    @pl.when(step == 0)
    def _():
        _gather_rows(src_hbm, cur_idx, n, dst(0), sem.at[0], wait=False)

    _gather_rows(src_hbm, cur_idx, n, dst(slot), sem.at[slot], wait=True)
    _gather_rows(src_hbm, next_idx, n, dst(1 - slot), sem.at[1 - slot], wait=False, inline=True)
    return slot


def _drain_gather(src_hbm, next_idx, n, dst, sem):
    step = pl.program_id(0)
    slot = step % 2

    @pl.when(step == pl.num_programs(0) - 1)
    def _():
        _gather_rows(src_hbm, next_idx, n, dst(1 - slot), sem.at[1 - slot], wait=True)


def _expert_kernel(be_ref, rt_ref, rtn_ref, h_hbm, w1_ref, w3_ref, w2_ref, o_ref, xbuf, sem):
    dst = lambda s: (lambda q: xbuf.at[s, pl.ds(q, 1), :])
    slot = _prefetched_gather(h_hbm, rt_ref, rtn_ref, MOE_BLOCK, dst, sem)
    x = xbuf[slot].astype(BF16)
    h1 = jnp.dot(x, w1_ref[...], preferred_element_type=F32)
    h3 = jnp.dot(x, w3_ref[...], preferred_element_type=F32)
    hid = (_silu(h1) * h3).astype(BF16)
    o_ref[...] = jnp.dot(hid, w2_ref[...], preferred_element_type=F32)
    _drain_gather(h_hbm, rtn_ref, MOE_BLOCK, dst, sem)


def _experts(h2, row_tok, w1, w3, w2, block_e):
    n_rows = row_tok.shape[0]
    n_blocks = n_rows // MOE_BLOCK
    rt = row_tok.reshape(n_blocks, 1, MOE_BLOCK)
    wspec = lambda a, b: pl.BlockSpec((None, a, b), lambda i, be: (be[i], 0, 0))
    ispec = lambda f: pl.BlockSpec((None, 1, MOE_BLOCK), lambda i, be: (f(i), 0, 0), memory_space=pltpu.SMEM)
    return pl.pallas_call(
        _expert_kernel,
        out_shape=jax.ShapeDtypeStruct((n_rows, D), F32),
        grid_spec=pltpu.PrefetchScalarGridSpec(
            num_scalar_prefetch=1,
            grid=(n_blocks,),
            in_specs=[ispec(lambda i: i), ispec(lambda i: jnp.minimum(i + 1, n_blocks - 1)),
                      pl.BlockSpec(memory_space=pl.ANY),
                      wspec(D, D_EXP), wspec(D, D_EXP), wspec(D_EXP, D)],
            out_specs=pl.BlockSpec((MOE_BLOCK, D), lambda i, be: (i, 0)),
            scratch_shapes=[pltpu.VMEM((2, MOE_BLOCK, D), F32), pltpu.SemaphoreType.DMA((2,))],
        ),
        compiler_params=_cparams(("arbitrary",)),
        name="experts",
    )(block_e, rt, rt, h2, w1, w3, w2)


def _combine_kernel(dc_ref, dn_ref, x_ref, tw_ref, gate_ref, nf_ref, yb_hbm, *rest, lay, final):
    if final:
        op_ref, os_ref, ybuf, sem = rest
    else:
        o_ref, ybuf, sem = rest
    i = pl.program_id(0)
    n_q = RB * TOP_K
    assert TOP_K == 2
    dst = lambda s: (lambda q: ybuf.at[s, q & 1, pl.ds(q >> 1, 1), :])
    slot = _prefetched_gather(yb_hbm, dc_ref, dn_ref, n_q, dst, sem)
    tw = tw_ref[...]
    y = x_ref[...] + gate_ref[...] * (ybuf[slot, 0] * tw[:, 0:1] + ybuf[slot, 1] * tw[:, 1:2])
    if not final:
        o_ref[...] = y
    else:
        y = _rms(y, nf_ref[...])

        @pl.when(i < lay.nb_ctx)
        def _():
            op_ref[...] = y

        @pl.when(i >= lay.nb_ctx)
        def _():
            os_ref[...] = y

    _drain_gather(yb_hbm, dn_ref, n_q, dst, sem)


def _combine(lay, x, yb, dest, tw, mod3, norm_f, *, final):
    mrow = lambda i: _mod_row(lay, i * RB)
    spec = pl.BlockSpec((RB, D), lambda i: (i, 0))
    n_q = RB * TOP_K
    dq = dest.reshape(lay.nb, 1, n_q)
    ispec = lambda f: pl.BlockSpec((None, 1, n_q), lambda i: (f(i), 0, 0), memory_space=pltpu.SMEM)
    if final:
        out_shape = (jax.ShapeDtypeStruct((lay.m_ctx, D), F32), jax.ShapeDtypeStruct((lay.m - lay.m_ctx, D), F32))
        out_specs = (pl.BlockSpec((RB, D), lambda i: (jnp.minimum(i, lay.nb_ctx - 1), 0)),
                     pl.BlockSpec((RB, D), lambda i: (jnp.maximum(i - lay.nb_ctx, 0), 0)))
    else:
        out_shape, out_specs = jax.ShapeDtypeStruct((lay.m, D), F32), spec
    return pl.pallas_call(
        functools.partial(_combine_kernel, lay=lay, final=final),
        out_shape=out_shape,
        grid=(lay.nb,),
        in_specs=[ispec(lambda i: i), ispec(lambda i: jnp.minimum(i + 1, lay.nb - 1)), spec,
                  pl.BlockSpec((RB, TOP_K), lambda i: (i, 0)),
                  pl.BlockSpec((None, 1, D), lambda i: (mrow(i) * 6 + 5, 0, 0)),
                  pl.BlockSpec((1, D), lambda i: (0, 0)),
                  pl.BlockSpec(memory_space=pl.ANY)],
        out_specs=out_specs,
        scratch_shapes=[pltpu.VMEM((2, TOP_K, RB, D), F32), pltpu.SemaphoreType.DMA((2,))],
        compiler_params=_cparams(("arbitrary",)),
        name="final" if final else "combine",
    )(dq, dq, x, tw, mod3, norm_f, yb)


def _route(logits, router_bias, n_tok):
    n_asg = n_tok * TOP_K
    scores = jax.nn.sigmoid(logits.T)
    biased = scores + router_bias.astype(F32)[:, None]
    bg = biased.reshape(N_GRP, EPG, n_tok)
    pair_sums = [bg[:, a] + bg[:, b] for a in range(EPG) for b in range(a + 1, EPG)]
    group_score = functools.reduce(jnp.maximum, pair_sums)
    best_group = jnp.argmax(group_score, axis=0)
    sel = (jnp.arange(N_GRP)[:, None, None] == best_group[None, None, :])
    vb = jnp.sum(jnp.where(sel, bg, 0.0), axis=0)
    vs = jnp.sum(jnp.where(sel, scores.reshape(N_GRP, EPG, n_tok), 0.0), axis=0)
    loc = jnp.arange(EPG)[:, None]
    i1 = jnp.argmax(vb, axis=0)
    i2 = jnp.argmax(jnp.where(loc == i1[None, :], -jnp.inf, vb), axis=0)
    top_loc = jnp.stack([i1, i2], axis=0)
    top_idx = (best_group[None, :] * EPG + top_loc).astype(jnp.int32)
    top_w = jnp.sum(jnp.where(loc[None, :, :] == top_loc[:, None, :], vs[None, :, :], 0.0), axis=1)
    top_w = top_w / jnp.sum(top_w, axis=0, keepdims=True)
    ex = jnp.arange(N_EXP, dtype=jnp.int32)[None, :, None]
    hit = top_idx[:, None, :] == ex
    cb = MOE_BLOCK
    oh = (hit[0] | hit[1]).astype(F32).reshape(N_EXP, n_tok // cb, cb)
    tri = (jnp.arange(cb)[:, None] < jnp.arange(cb)[None, :]).astype(F32)
    within = jnp.einsum('ebs,st->ebt', oh, tri)
    bsum = oh.sum(-1)
    counts = bsum.sum(-1).astype(jnp.int32)
    padded = (counts + MOE_BLOCK - 1) // MOE_BLOCK * MOE_BLOCK
    padded_end = jnp.cumsum(padded)
    padded_start = padded_end - padded
    base = jnp.cumsum(bsum, axis=1) - bsum + padded_start.astype(F32)[:, None]
    pos_e = (within + base[:, :, None]).reshape(N_EXP, n_tok)
    dest = jnp.sum(jnp.where(hit, pos_e[None], 0.0), axis=1).astype(jnp.int32)
    n_blocks = -(-n_asg // MOE_BLOCK) + N_EXP
    n_rows = n_blocks * MOE_BLOCK
    tok = jnp.broadcast_to(jnp.arange(n_tok, dtype=jnp.int32)[None, :], (TOP_K, n_tok))
    row_tok = jnp.zeros((n_rows,), jnp.int32).at[dest.reshape(-1)].set(tok.reshape(-1))
    block_start = jnp.arange(n_blocks, dtype=jnp.int32) * MOE_BLOCK
    block_e = jnp.minimum(jnp.searchsorted(padded_end, block_start, side='right'), N_EXP - 1).astype(jnp.int32)
    return row_tok, top_w.T, block_e, dest.T


def _sincos_2d(rows, cols, dim):
    quarter = dim // 4
    omega = 1.0 / (POS_BASE ** (jnp.arange(quarter, dtype=F32) / quarter))

    def axis_embed(n):
        ang = jnp.arange(n, dtype=F32)[:, None] * omega[None, :]
        return jnp.concatenate([jnp.sin(ang), jnp.cos(ang)], axis=-1)

    er = jnp.broadcast_to(axis_embed(rows)[:, None, :], (rows, cols, dim // 2))
    ec = jnp.broadcast_to(axis_embed(cols)[None, :, :], (rows, cols, dim // 2))
    return jnp.concatenate([er, ec], axis=-1).reshape(rows * cols, dim)


def _rearranged_w_in(w):
    sizes = (GLA_QK, GLA_QK, GLA_V, GLA_V, 2 * GLA_LR, D_INNER, CONV_DIM, 2 * SSD_HEADS, D, D)
    offs = [0]
    for s in sizes:
        offs.append(offs[-1] + s)
    q, k, v, g, dlr, z, xbc, dt, ga, gb = [w[:, offs[i]:offs[i + 1]] for i in range(len(sizes))]
    zpad = jnp.zeros((w.shape[0], LANE - GLA_LR), w.dtype)
    return jnp.concatenate([q, k, v, g, ga, z, xbc, gb, dt, dlr[:, :GLA_LR], zpad, dlr[:, GLA_LR:], zpad,
                            jnp.zeros((w.shape[0], LANE), w.dtype)], axis=1).astype(BF16)


def _pick_tile(lay, want):
    t = want
    while lay.m_ctx % t or lay.t_lat % t:
        t //= 2
    return t


def kernel(x_prompt, x_sample, c, state_gla, state_ssd, c_ctx, w_ada, b_ada, norm1, norm2, w_in, gla_dw2, gla_db, gla_norm, w_gla_b, conv_w, conv_b, a_log, dt_bias, d_skip, ssd_norm, w_ssd_b, w_out, w_router, router_bias, w_e1, w_e3, w_e2, norm_f):
    n_ctx, t_ctx, _ = x_prompt.shape
    n_lat, t_lat, _ = x_sample.shape
    lay = Layout(n_ctx, t_ctx, n_lat, t_lat)
    depth = w_in.shape[0]
    m = lay.m
    assert t_ctx % RB == 0 and t_lat % RB == 0 and t_lat % GRID_W == 0

    pos = _sincos_2d(t_lat // GRID_W, GRID_W, D)
    x = _embed(lay, x_prompt.reshape(lay.m_ctx, D), x_sample.reshape(n_lat * t_lat, D), pos)

    n_mod = 16
    cond = jnp.concatenate([c_ctx[None, :], c, jnp.zeros((n_mod - 1 - n_lat, D), F32)], axis=0)
    wr = jnp.pad(w_router, ((0, 0), (0, LANE - N_EXP)))

    tm_in = _pick_tile(lay, 1024)
    tm_a = _pick_tile(lay, 512)
    mrow = lambda tm: (lambda i: _mod_row(lay, i * tm))
    new_gla, new_ssd = [], []
    y_prompt = y_sample = None
    for l in range(depth):
        mod = _fused_mm("ada", n_mod, D, 6 * D, tm=n_mod, tn=1024, rows=[(cond, D, 0)], vecs=[],
                        w=w_ada[l].astype(BF16), evecs=[(b_ada[l].reshape(1, 1, 6 * D), lambda i: 0)],
                        prologue=_pro_silu, epilogue=_epi_bias, out_dtype=F32, rc=n_mod)
        mod3 = mod.reshape(n_mod * 6, 1, D)
        n1 = norm1[l].reshape(1, 1, D)
        mr = mrow(tm_in)
        proj = _fused_mm("in_proj", m, D, NP, tm=tm_in, tn=512, rows=[(x, D, 0)],
                         vecs=[(n1, D, lambda i: 0), (mod3, D, lambda i: mr(i) * 6 + 1),
                               (mod3, D, lambda i: mr(i) * 6 + 0)],
                         w=_rearranged_w_in(w_in[l]), prologue=_pro_norm_mod, epilogue=None, out_dtype=F32)

        dw2p = jnp.pad(gla_dw2[l], ((0, 0), (0, LANE - GLA_LR), (0, 0))).astype(BF16)
        db = gla_db[l].reshape(2, 1, GLA_QK)
        o_f, sg_f = _gla_scan(lay, proj, dw2p, db, state_gla, l, reverse=False, name="gla_fwd")
        o_b, sg_b = _gla_scan(lay, proj, dw2p, db, state_gla, l, reverse=True, name="gla_bwd")
        new_gla.append(jnp.stack([sg_f, sg_b], axis=1))
        m_a = _kmm("gla_out", m, GLA_V, D, tm=tm_a, tk=GLA_DV,
                   rows=[(o_f, 0), (o_b, 0), (proj, C_G // GLA_DV)], vecs=[gla_norm[l].reshape(1, GLA_V)],
                   w=w_gla_b[l].astype(BF16), erows=[(proj, C_GA // D)],
                   prologue=_pro_gla_out, epilogue=_epi_gate_a, out_dtype=F32)

        xbc = _conv(lay, proj, conv_w[l], conv_b[l].reshape(1, CONV_DIM))
        dtg, cumg = _dtprep(lay, proj, dt_bias[l].reshape(1, LANE), a_log[l].reshape(1, LANE))
        h0 = state_ssd[:, l].reshape(n_lat, 2, SSD_G, SSD_HPG, SSD_N, SSD_P)
        h0 = h0.transpose(0, 1, 2, 4, 3, 5).reshape(n_lat, 2, SSD_G, SSD_N, SSD_HPG * SSD_P)
        dsk = jnp.repeat(d_skip[l], SSD_P).reshape(1, D_INNER)
        y_f, hs_f = _ssd_scan(lay, xbc, dtg, cumg, h0, dsk, reverse=False, name="ssd_fwd")
        y_b, hs_b = _ssd_scan(lay, xbc, dtg, cumg, h0, dsk, reverse=True, name="ssd_bwd")
        hs = jnp.stack([hs_f, hs_b], axis=1).reshape(n_ctx, 2, SSD_G, SSD_N, SSD_HPG, SSD_P)
        new_ssd.append(hs.transpose(0, 1, 2, 4, 3, 5).reshape(n_ctx, 2, SSD_HEADS, SSD_N, SSD_P))
        merged = _kmm("ssd_out", m, D_INNER, D, tm=tm_a, tk=512,
                      rows=[(y_f, 0), (y_b, 0), (proj, C_Z // 512)], vecs=[ssd_norm[l].reshape(1, D_INNER)],
                      w=w_ssd_b[l].astype(BF16), erows=[(m_a, 0), (proj, C_GB // D)],
                      prologue=_pro_ssd_out, epilogue=_epi_gate_b, out_dtype=BF16)

        x = _fused_mm("mix_out", m, D, D, tm=tm_in, tn=512, rows=[(merged, D, 0)], vecs=[],
                      w=w_out[l].astype(BF16), erows=[(x, 0)], evecs=[(mod3, lambda i: mr(i) * 6 + 2)],
                      prologue=None, epilogue=_epi_resid, out_dtype=F32)

        h2, logits = _norm_router(lay, x, norm2[l].reshape(1, 1, D), mod3, wr)
        row_tok, top_w, block_e, pos_tk = _route(logits[:, :N_EXP], router_bias, m)
        yb = _experts(h2, row_tok, w_e1[l].astype(BF16), w_e3[l].astype(BF16), w_e2[l].astype(BF16), block_e)
        if l + 1 < depth:
            x = _combine(lay, x, yb, pos_tk, top_w, mod3, norm_f.reshape(1, D), final=False)
        else:
            y_prompt, y_sample = _combine(lay, x, yb, pos_tk, top_w, mod3, norm_f.reshape(1, D), final=True)

    return (y_prompt.reshape(n_ctx, t_ctx, D), y_sample.reshape(n_lat, t_lat, D),
            jnp.stack(new_gla, axis=1), jnp.stack(new_ssd, axis=1))
```

```python
import functools
import math
from typing import NamedTuple

import jax
import jax.numpy as jnp
from jax import lax
from jax.experimental import pallas as pl
from jax.experimental.pallas import tpu as pltpu

F32 = jnp.float32
BF16 = jnp.bfloat16

D = 2048
GRID_W = 64
POS_BASE = 10000.0
EPS = 1e-6
GLA_H = 4
GLA_DK = D // 2 // GLA_H
GLA_DV = D // GLA_H
GLA_QK = GLA_H * GLA_DK
GLA_V = GLA_H * GLA_DV
GLA_LR = 16
GLA_TAU = 16.0
GLA_C = 64
D_INNER = 2 * D
SSD_P = 64
SSD_HEADS = D_INNER // SSD_P
SSD_N = 128
SSD_G = 8
SSD_HPG = SSD_HEADS // SSD_G
SSD_BC = SSD_G * SSD_N
CONV_DIM = D_INNER + 2 * SSD_BC
D_CONV = 5
SSD_C = 128
N_EXP = 16
N_GRP = 4
EPG = N_EXP // N_GRP
TOP_K = 2
D_EXP = D // 2
MOE_BLOCK = 256

LANE = 128
SUBLANE = 8
RB = 256
VMEM_LIMIT = 58 * 1024 * 1024

C_Q = 0
C_K = C_Q + GLA_QK
C_V = C_K + GLA_QK
C_G = C_V + GLA_V
C_GA = C_G + GLA_V
C_Z = C_GA + D
C_XBC = C_Z + D_INNER
C_B = C_XBC + D_INNER
C_CM = C_B + SSD_BC
C_GB = C_XBC + CONV_DIM
C_DT = C_GB + D
C_DLR = C_DT + 2 * SSD_HEADS
NP = C_DLR + 3 * LANE


class Layout(NamedTuple):
    n_ctx: int
    t_ctx: int
    n_lat: int
    t_lat: int

    @property
    def m_ctx(self):
        return self.n_ctx * self.t_ctx

    @property
    def m(self):
        return self.m_ctx + self.n_lat * self.t_lat

    @property
    def nb(self):
        return self.m // RB

    @property
    def nb_ctx(self):
        return self.m_ctx // RB

    @property
    def bps_ctx(self):
        return self.t_ctx // RB

    @property
    def bps_lat(self):
        return self.t_lat // RB


def _seq_info(lay, blk):
    is_ctx = blk < lay.nb_ctx
    lb = jnp.maximum(blk - lay.nb_ctx, 0)
    cb = jnp.minimum(blk, lay.nb_ctx - 1)
    pos = jnp.where(is_ctx, cb % lay.bps_ctx, lb % lay.bps_lat)
    bps = jnp.where(is_ctx, lay.bps_ctx, lay.bps_lat)
    return is_ctx, cb // lay.bps_ctx, jnp.minimum(lb // lay.bps_lat, lay.n_lat - 1), pos, bps


def _mod_row(lay, row_start):
    return jnp.where(row_start < lay.m_ctx, 0, 1 + jnp.maximum(row_start - lay.m_ctx, 0) // lay.t_lat)


def _cparams(sem):
    return pltpu.CompilerParams(dimension_semantics=sem, vmem_limit_bytes=VMEM_LIMIT)


def _silu(x):
    return x * jax.nn.sigmoid(x)


def _softplus(x):
    return jnp.maximum(x, 0.0) + jnp.log1p(jnp.exp(-jnp.abs(x)))


def _cumsum_rows(x, n, reverse):
    rows = lax.broadcasted_iota(jnp.int32, x.shape, 0)
    s = 1
    while s < n:
        if reverse:
            x = x + jnp.where(rows < n - s, pltpu.roll(x, n - s, 0), 0.0)
        else:
            x = x + jnp.where(rows >= s, pltpu.roll(x, s, 0), 0.0)
        s *= 2
    return x


NT = (((1,), (1,)), ((), ()))
TN = (((0,), (0,)), ((), ()))


def _embed_kernel(xp_ref, xs_ref, pos_ref, o_ref, *, lay):
    rb = pl.program_id(0)

    @pl.when(rb < lay.nb_ctx)
    def _():
        o_ref[...] = xp_ref[...]

    @pl.when(rb >= lay.nb_ctx)
    def _():
        o_ref[...] = xs_ref[...] + pos_ref[...]


def _embed(lay, xp, xs, pos):
    spec = lambda f: pl.BlockSpec((RB, D), f)
    return pl.pallas_call(
        functools.partial(_embed_kernel, lay=lay),
        out_shape=jax.ShapeDtypeStruct((lay.m, D), F32),
        grid=(lay.nb,),
        in_specs=[spec(lambda i: (jnp.minimum(i, lay.nb_ctx - 1), 0)),
                  spec(lambda i: (jnp.maximum(i - lay.nb_ctx, 0), 0)),
                  spec(lambda i: (jnp.maximum(i - lay.nb_ctx, 0) % lay.bps_lat, 0))],
        out_specs=spec(lambda i: (i, 0)),
        compiler_params=_cparams(("arbitrary",)),
        name="embed",
    )(xp, xs, pos)


def _fused_mm_kernel(*refs, n_rows, n_vecs, n_erows, n_evecs, prologue, epilogue, tm, rc):
    rows = refs[:n_rows]
    vecs = refs[n_rows:n_rows + n_vecs]
    w_ref = refs[n_rows + n_vecs]
    p = n_rows + n_vecs + 1
    erows = refs[p:p + n_erows]
    evecs = refs[p + n_erows:p + n_erows + n_evecs]
    o_ref = refs[p + n_erows + n_evecs]

    if prologue is None:
        lhs = rows[0][...]
    else:
        h_ref = refs[p + n_erows + n_evecs + 1]

        @pl.when(pl.program_id(1) == 0)
        def _():
            vs = [v[...] for v in vecs]

            def chunk(c, carry):
                sl = pl.ds(pl.multiple_of(c * rc, rc), rc)
                h_ref[sl, :] = prologue([r[sl, :] for r in rows], vs).astype(BF16)
                return carry

            lax.fori_loop(0, tm // rc, chunk, 0)

        lhs = h_ref[...]
    acc = jnp.dot(lhs, w_ref[...], preferred_element_type=F32)
    if epilogue is not None:
        acc = epilogue(acc, [e[...] for e in erows], [v[...] for v in evecs])
    o_ref[...] = acc.astype(o_ref.dtype)


def _fused_mm(name, m, k, n, *, tm, tn, rows, vecs, w, w_col0=0, erows=(), evecs=(), prologue, epilogue,
              out_dtype, rc=32):
    in_specs, args = [], []
    for arr, width, cb in rows:
        in_specs.append(pl.BlockSpec((tm, width), lambda i, j, cb=cb: (i, cb)))
        args.append(arr)
    for arr, width, fn in vecs:
        in_specs.append(pl.BlockSpec((None, 1, width), lambda i, j, fn=fn: (fn(i), 0, 0)))
        args.append(arr)
    in_specs.append(pl.BlockSpec((k, tn), lambda i, j: (0, w_col0 + j)))
    args.append(w)
    for arr, cb0 in erows:
        in_specs.append(pl.BlockSpec((tm, tn), lambda i, j, cb0=cb0: (i, cb0 + j)))
        args.append(arr)
    for arr, fn in evecs:
        in_specs.append(pl.BlockSpec((None, 1, tn), lambda i, j, fn=fn: (fn(i), 0, j)))
        args.append(arr)
    scratch = [] if prologue is None else [pltpu.VMEM((tm, k), BF16)]
    kern = functools.partial(_fused_mm_kernel, n_rows=len(rows), n_vecs=len(vecs), n_erows=len(erows),
                             n_evecs=len(evecs), prologue=prologue, epilogue=epilogue, tm=tm, rc=rc)
    return pl.pallas_call(
        kern,
        out_shape=jax.ShapeDtypeStruct((m, n), out_dtype),
        grid=(m // tm, n // tn),
        in_specs=in_specs,
        out_specs=pl.BlockSpec((tm, tn), lambda i, j: (i, j)),
        scratch_shapes=scratch,
        compiler_params=_cparams(("arbitrary", "arbitrary")),
        name=name,
    )(*args)


def _rms(x, gain):
    return x * lax.rsqrt(jnp.mean(x * x, axis=-1, keepdims=True) + EPS) * gain


def _pro_norm_mod(rows, vecs):
    (x,), (gain, scale, shift) = rows, vecs
    return _rms(x, gain) * (1.0 + scale) + shift


def _pro_silu(rows, vecs):
    return _silu(rows[0])


def _epi_bias(acc, erows, evecs):
    return acc + evecs[0]


def _epi_resid(acc, erows, evecs):
    return erows[0] + evecs[0] * acc


def _kmm_kernel(*refs, n_rows, n_vecs, n_erows, prologue, epilogue, tm, rc, nk):
    rows = refs[:n_rows]
    vecs = refs[n_rows:n_rows + n_vecs]
    w_ref = refs[n_rows + n_vecs]
    p = n_rows + n_vecs + 1
    erows = refs[p:p + n_erows]
    o_ref, acc_ref, lhs_ref, ss_ref = refs[p + n_erows:p + n_erows + 4]
    k = pl.program_id(1)

    @pl.when(k == 0)
    def _():
        acc_ref[...] = jnp.zeros_like(acc_ref)
        ss_ref[...] = jnp.zeros_like(ss_ref)

    vs = [v[...] for v in vecs]

    def chunk(c, carry):
        sl = pl.ds(pl.multiple_of(c * rc, rc), rc)
        lhs, ss = prologue([r[sl, :] for r in rows], vs)
        lhs_ref[sl, :] = lhs.astype(BF16)
        if ss is not None:
            ss_ref[sl, :] += ss
        return carry

    lax.fori_loop(0, tm // rc, chunk, 0)
    acc_ref[...] += jnp.dot(lhs_ref[...], w_ref[...], preferred_element_type=F32)

    @pl.when(k == nk - 1)
    def _():
        o_ref[...] = epilogue(acc_ref[...], ss_ref[...], [e[...] for e in erows]).astype(o_ref.dtype)


def _kmm(name, m, k, n, *, tm, tk, rows, vecs, w, erows, prologue, epilogue, out_dtype, rc=64):
    in_specs, args = [], []
    for arr, cb0 in rows:
        in_specs.append(pl.BlockSpec((tm, tk), lambda i, kk, cb0=cb0: (i, cb0 + kk)))
        args.append(arr)
    for arr in vecs:
        in_specs.append(pl.BlockSpec((1, tk), lambda i, kk: (0, kk)))
        args.append(arr)
    in_specs.append(pl.BlockSpec((tk, n), lambda i, kk: (kk, 0)))
    args.append(w)
    for arr, cb in erows:
        in_specs.append(pl.BlockSpec((tm, n), lambda i, kk, cb=cb: (i, cb)))
        args.append(arr)
    kern = functools.partial(_kmm_kernel, n_rows=len(rows), n_vecs=len(vecs), n_erows=len(erows),
                             prologue=prologue, epilogue=epilogue, tm=tm, rc=rc, nk=k // tk)
    return pl.pallas_call(
        kern,
        out_shape=jax.ShapeDtypeStruct((m, n), out_dtype),
        grid=(m // tm, k // tk),
        in_specs=in_specs,
        out_specs=pl.BlockSpec((tm, n), lambda i, kk: (i, 0)),
        scratch_shapes=[pltpu.VMEM((tm, n), F32), pltpu.VMEM((tm, tk), BF16), pltpu.VMEM((tm, 1), F32)],
        compiler_params=_cparams(("arbitrary", "arbitrary")),
        name=name,
    )(*args)


def _pro_gla_out(rows, vecs):
    (of, ob, g), (gain,) = rows, vecs
    return _rms(of + ob, gain) * _silu(g), None


def _epi_gate_a(acc, ss, erows):
    return jax.nn.sigmoid(erows[0]) * acc


def _pro_ssd_out(rows, vecs):
    (yf, yb, z), (gain,) = rows, vecs
    u = (yf + yb) * _silu(z)
    return u * gain, jnp.sum(u * u, axis=-1, keepdims=True)


def _epi_gate_b(acc, ss, erows):
    rstd = lax.rsqrt(ss * (1.0 / D_INNER) + EPS)
    return erows[0] + jax.nn.sigmoid(erows[1]) * (acc * rstd)


def _gla_kernel(q_ref, k_ref, v_ref, dlr_ref, dw_ref, db_ref, s0_ref, o_ref, sf_ref, s_ref, *, lay, reverse):
    i = pl.program_id(1)
    blk = (lay.nb - 1 - i) if reverse else i
    is_ctx, _, _, pos, bps = _seq_info(lay, blk)
    first = (pos == bps - 1) if reverse else (pos == 0)
    last = (pos == 0) if reverse else (pos == bps - 1)

    @pl.when(first & is_ctx)
    def _():
        s_ref[...] = jnp.zeros_like(s_ref)

    @pl.when(first & jnp.logical_not(is_ctx))
    def _():
        s_ref[...] = s0_ref[...]

    x = jnp.dot(dlr_ref[...].astype(BF16), dw_ref[...], preferred_element_type=F32) + db_ref[...]
    log_a = (jnp.minimum(x, 0.0) - jnp.log1p(jnp.exp(-jnp.abs(x)))) * (1.0 / GLA_TAU)

    r = lax.broadcasted_iota(jnp.int32, (GLA_C, GLA_C), 0)
    c = lax.broadcasted_iota(jnp.int32, (GLA_C, GLA_C), 1)
    mask = (c >= r) if reverse else (c <= r)
    r2 = lax.broadcasted_iota(jnp.int32, (2 * GLA_C, GLA_C), 0)
    c2i = lax.broadcasted_iota(jnp.int32, (2 * GLA_C, GLA_C), 1)
    mask_xa = ((c2i >= r2) if reverse else (c2i <= r2)) | (r2 >= GLA_C)
    c2 = 2 * GLA_C
    for pi in (range(RB // c2 - 1, -1, -1) if reverse else range(RB // c2)):
        lo_sl = slice(pi * c2, pi * c2 + GLA_C)
        hi_sl = slice(pi * c2 + GLA_C, (pi + 1) * c2)
        sl_a, sl_b = (hi_sl, lo_sl) if reverse else (lo_sl, hi_sl)
        end_row = (lambda b: b[0:1, :]) if reverse else (lambda b: b[GLA_C - 1:GLA_C, :])
        b_a = _cumsum_rows(log_a[sl_a], GLA_C, reverse)
        b_b = _cumsum_rows(log_a[sl_b], GLA_C, reverse)
        end_a, end_b = end_row(b_a), end_row(b_b)
        q_a = q_ref[sl_a, :] * (GLA_DK ** -0.5)
        q_b = q_ref[sl_b, :] * (GLA_DK ** -0.5)
        k_a, k_b = k_ref[sl_a, :], k_ref[sl_b, :]
        v_a, v_b = v_ref[sl_a, :].astype(BF16), v_ref[sl_b, :].astype(BF16)
        qe_a = (q_a * jnp.exp(b_a)).astype(BF16)
        qe_b = (q_b * jnp.exp(b_b)).astype(BF16)
        qs_b = (q_b * jnp.exp(b_b) * jnp.exp(end_a)).astype(BF16)
        ke_a = (k_a * jnp.exp(-b_a)).astype(BF16)
        ke_b = (k_b * jnp.exp(-b_b)).astype(BF16)
        qq = jnp.concatenate([qe_a, qs_b], axis=0)
        att_xa = jnp.where(mask_xa, lax.dot_general(qq, ke_a, NT, preferred_element_type=F32), 0.0)
        att_bb = jnp.where(mask, lax.dot_general(qe_b, ke_b, NT, preferred_element_type=F32), 0.0)
        s = s_ref[...]
        o_ab = (jnp.dot(att_xa.astype(BF16), v_a, preferred_element_type=F32)
                + jnp.dot(qq, s.astype(BF16), preferred_element_type=F32))
        o_ref[sl_a, :] = o_ab[:GLA_C]
        o_ref[sl_b, :] = o_ab[GLA_C:] + jnp.dot(att_bb.astype(BF16), v_b, preferred_element_type=F32)
        kd_a = (k_a * jnp.exp(end_a - b_a) * jnp.exp(end_b)).astype(BF16)
        kd_b = (k_b * jnp.exp(end_b - b_b)).astype(BF16)
        e_col = jnp.exp(jnp.transpose(jnp.broadcast_to(end_a + end_b, (LANE, GLA_DK))))
        upd = lax.dot_general(jnp.concatenate([kd_a, kd_b], axis=0), jnp.concatenate([v_a, v_b], axis=0), TN,
                              preferred_element_type=F32)
        s_ref[...] = s * jnp.concatenate([e_col] * (GLA_DV // LANE), axis=1) + upd

    @pl.when(last & is_ctx)
    def _():
        sf_ref[...] = s_ref[...]


def _gla_scan(lay, proj, dw2p, db, s0, layer, *, reverse, name):
    d = 1 if reverse else 0
    blk = (lambda i: lay.nb - 1 - i) if reverse else (lambda i: i)
    cseq = lambda i: _seq_info(lay, blk(i))[1]
    lseq = lambda i: _seq_info(lay, blk(i))[2]
    return pl.pallas_call(
        functools.partial(_gla_kernel, lay=lay, reverse=reverse),
        out_shape=(jax.ShapeDtypeStruct((lay.m, GLA_V), F32),
                   jax.ShapeDtypeStruct((lay.n_ctx, GLA_H, GLA_DK, GLA_DV), F32)),
        grid=(GLA_H, lay.nb),
        in_specs=[
            pl.BlockSpec((RB, GLA_DK), lambda h, i: (blk(i), C_Q // GLA_DK + h)),
            pl.BlockSpec((RB, GLA_DK), lambda h, i: (blk(i), C_K // GLA_DK + h)),
            pl.BlockSpec((RB, GLA_DV), lambda h, i: (blk(i), C_V // GLA_DV + h)),
            pl.BlockSpec((RB, LANE), lambda h, i: (blk(i), C_DLR // LANE + d)),
            pl.BlockSpec((None, LANE, GLA_DK), lambda h, i: (d, 0, h)),
            pl.BlockSpec((None, 1, GLA_DK), lambda h, i: (d, 0, h)),
            pl.BlockSpec((None, None, None, None, GLA_DK, GLA_DV), lambda h, i: (lseq(i), layer, d, h, 0, 0)),
        ],
        out_specs=(pl.BlockSpec((RB, GLA_DV), lambda h, i: (blk(i), h)),
                   pl.BlockSpec((None, None, GLA_DK, GLA_DV), lambda h, i: (cseq(i), h, 0, 0))),
        scratch_shapes=[pltpu.VMEM((GLA_DK, GLA_DV), F32)],
        compiler_params=_cparams(("arbitrary", "arbitrary")),
        name=name,
    )(proj, proj, proj, proj, dw2p, db, s0)


CONV_TC = 2048
CONV_SC = 512


def _conv_kernel(prev_ref, cur_ref, next_ref, w_ref, b_ref, o_ref, *, lay):
    rb = pl.program_id(0)
    _, _, _, pos, bps = _seq_info(lay, rb)
    n = RB + 2 * SUBLANE
    for s in range(CONV_TC // CONV_SC):
        cs = slice(s * CONV_SC, (s + 1) * CONV_SC)
        prev = jnp.where(pos > 0, prev_ref[:, cs], 0.0)
        nxt = jnp.where(pos < bps - 1, next_ref[:, cs], 0.0)
        ext = jnp.concatenate([prev, cur_ref[:, cs], nxt], axis=0)
        w = w_ref[:, cs]
        acc = jnp.broadcast_to(b_ref[:, cs], (RB, CONV_SC))
        for j in range(D_CONV):
            off = j - D_CONV // 2
            sh = ext if off == 0 else pltpu.roll(ext, (-off) % n, 0)
            acc = acc + w[j:j + 1, :] * sh[SUBLANE:SUBLANE + RB, :]
        o_ref[:, cs] = _silu(acc)


def _conv(lay, proj, cw, cb):
    c0 = C_XBC // CONV_TC
    rpb = RB // SUBLANE
    return pl.pallas_call(
        functools.partial(_conv_kernel, lay=lay),
        out_shape=jax.ShapeDtypeStruct((lay.m, CONV_DIM), F32),
        grid=(lay.nb, CONV_DIM // CONV_TC),
        in_specs=[
            pl.BlockSpec((SUBLANE, CONV_TC), lambda i, j: (jnp.maximum(i * rpb - 1, 0), c0 + j)),
            pl.BlockSpec((RB, CONV_TC), lambda i, j: (i, c0 + j)),
            pl.BlockSpec((SUBLANE, CONV_TC), lambda i, j: (jnp.minimum((i + 1) * rpb, lay.m // SUBLANE - 1), c0 + j)),
            pl.BlockSpec((D_CONV, CONV_TC), lambda i, j: (0, j)),
            pl.BlockSpec((1, CONV_TC), lambda i, j: (0, j)),
        ],
        out_specs=pl.BlockSpec((RB, CONV_TC), lambda i, j: (i, j)),
        compiler_params=_cparams(("arbitrary", "arbitrary")),
        name="conv",
    )(proj, proj, proj, cw, cb)


def _dtprep_kernel(raw_ref, bias_ref, alog_ref, dt_ref, cum_ref):
    dt = _softplus(raw_ref[...] + bias_ref[...])
    dta = dt * (-jnp.exp(alog_ref[...]))
    lane = lax.broadcasted_iota(jnp.int32, (SSD_C, LANE), 1)
    cums = []
    for ci in range(RB // SSD_C):
        t = dta[ci * SSD_C:(ci + 1) * SSD_C]
        cums.append(jnp.where(lane < SSD_HEADS, _cumsum_rows(t, SSD_C, False), _cumsum_rows(t, SSD_C, True)))
    cum = jnp.concatenate(cums, axis=0)
    for d in range(2):
        for g in range(SSD_G):
            base = d * SSD_HEADS + g * SSD_HPG
            sh = (LANE - base) % LANE
            dt_ref[d, g] = dt if sh == 0 else pltpu.roll(dt, sh, 1)
            cum_ref[d, g] = cum if sh == 0 else pltpu.roll(cum, sh, 1)


def _dtprep(lay, proj, dt_bias, a_log):
    out = jax.ShapeDtypeStruct((2, SSD_G, lay.m, LANE), F32)
    ospec = pl.BlockSpec((2, SSD_G, RB, LANE), lambda i: (0, 0, i, 0))
    vspec = pl.BlockSpec((1, LANE), lambda i: (0, 0))
    return pl.pallas_call(
        _dtprep_kernel,
        out_shape=(out, out),
        grid=(lay.nb,),
        in_specs=[pl.BlockSpec((RB, LANE), lambda i: (i, C_DT // LANE)), vspec, vspec],
        out_specs=(ospec, ospec),
        compiler_params=_cparams(("arbitrary",)),
        name="dtprep",
    )(proj, dt_bias, a_log)


def _ssd_kernel(x_ref, b_ref, c_ref, dt_ref, cum_ref, h0_ref, dsk_ref, y_ref, hf_ref, h_ref, *, lay, reverse):
    i = pl.program_id(1)
    blk = (lay.nb - 1 - i) if reverse else i
    is_ctx, _, _, pos, bps = _seq_info(lay, blk)
    first = (pos == bps - 1) if reverse else (pos == 0)
    last = (pos == 0) if reverse else (pos == bps - 1)

    @pl.when(first & is_ctx)
    def _():
        h_ref[...] = jnp.zeros_like(h_ref)

    @pl.when(first & jnp.logical_not(is_ctx))
    def _():
        h_ref[...] = h0_ref[...]

    r = lax.broadcasted_iota(jnp.int32, (SSD_C, SSD_C), 0)
    c = lax.broadcasted_iota(jnp.int32, (SSD_C, SSD_C), 1)
    mask = (c >= r) if reverse else (c <= r)
    lo = c < SSD_P
    gw = SSD_HPG * SSD_P
    spread = jnp.where(lax.broadcasted_iota(jnp.int32, (LANE, gw), 0)
                       == (lax.broadcasted_iota(jnp.int32, (LANE, gw), 1) >> int(math.log2(SSD_P))),
                       1.0, 0.0).astype(BF16)
    n_chunks = RB // SSD_C
    order = range(n_chunks - 1, -1, -1) if reverse else range(n_chunks)
    n_pairs = SSD_HPG // 2
    for ci in order:
        sl = slice(ci * SSD_C, (ci + 1) * SSD_C)
        x = x_ref[sl, :]
        bm = b_ref[sl, :].astype(BF16)
        cm = c_ref[sl, :].astype(BF16)
        dt_x = sum(jnp.dot(piece, spread, preferred_element_type=F32) for piece in _split3(dt_ref[sl, :]))
        cum = cum_ref[sl, :]
        cum_t = jnp.transpose(cum)
        cb = lax.dot_general(cm, bm, NT, preferred_element_type=F32)
        h = h_ref[...]
        y_inter = jnp.dot(cm, h.astype(BF16), preferred_element_type=F32)
        e_idx = 0 if reverse else SSD_C - 1
        ys, xdes, decs = [], [], []
        for j in range(n_pairs):
            cols, ls = [], []
            for rr in (2 * j, 2 * j + 1):
                col = jnp.broadcast_to(cum[:, rr:rr + 1], (SSD_C, SSD_C))
                seg = col - cum_t[rr:rr + 1, :]
                ls.append((jnp.exp(jnp.where(mask, seg, -1e30)) * cb).astype(BF16))
                cols.append(col)
            cp = jnp.where(lo, cols[0], cols[1])
            dtp = dt_x[:, j * LANE:(j + 1) * LANE]
            xp = x[:, j * LANE:(j + 1) * LANE]
            xdt = xp * dtp
            rhs = jnp.concatenate([jnp.where(lo, xdt, 0.0), jnp.where(lo, 0.0, xdt)], axis=0).astype(BF16)
            lhs = jnp.concatenate(ls, axis=1)
            y = jnp.dot(lhs, rhs, preferred_element_type=F32)
            y = y + jnp.exp(cp) * y_inter[:, j * LANE:(j + 1) * LANE]
            if not reverse:
                y = y + dsk_ref[:, j * LANE:(j + 1) * LANE] * xp
            ys.append(y)
            edge = cp[e_idx:e_idx + 1, :]
            xdes.append((xdt * jnp.exp(edge - cp)).astype(BF16))
            decs.append(jnp.exp(edge))
        y_ref[sl, :] = jnp.concatenate(ys, axis=1)
        upd = lax.dot_general(bm, jnp.concatenate(xdes, axis=1), TN, preferred_element_type=F32)
        h_ref[...] = h * jnp.concatenate(decs, axis=1) + upd

    @pl.when(last & is_ctx)
    def _():
        hf_ref[...] = h_ref[...]


def _ssd_scan(lay, xbc, dtg, cumg, h0, dsk, *, reverse, name):
    d = 1 if reverse else 0
    blk = (lambda i: lay.nb - 1 - i) if reverse else (lambda i: i)
    cseq = lambda i: _seq_info(lay, blk(i))[1]
    lseq = lambda i: _seq_info(lay, blk(i))[2]
    gw = SSD_HPG * SSD_P
    return pl.pallas_call(
        functools.partial(_ssd_kernel, lay=lay, reverse=reverse),
        out_shape=(jax.ShapeDtypeStruct((lay.m, D_INNER), F32),
                   jax.ShapeDtypeStruct((lay.n_ctx, SSD_G, SSD_N, gw), F32)),
        grid=(SSD_G, lay.nb),
        in_specs=[
            pl.BlockSpec((RB, gw), lambda g, i: (blk(i), g)),
            pl.BlockSpec((RB, SSD_N), lambda g, i: (blk(i), D_INNER // SSD_N + g)),
            pl.BlockSpec((RB, SSD_N), lambda g, i: (blk(i), (D_INNER + SSD_BC) // SSD_N + g)),
            pl.BlockSpec((None, None, RB, LANE), lambda g, i: (d, g, blk(i), 0)),
            pl.BlockSpec((None, None, RB, LANE), lambda g, i: (d, g, blk(i), 0)),
            pl.BlockSpec((None, None, None, SSD_N, gw), lambda g, i: (lseq(i), d, g, 0, 0)),
            pl.BlockSpec((1, gw), lambda g, i: (0, g)),
        ],
        out_specs=(pl.BlockSpec((RB, gw), lambda g, i: (blk(i), g)),
                   pl.BlockSpec((None, None, SSD_N, gw), lambda g, i: (cseq(i), g, 0, 0))),
        scratch_shapes=[pltpu.VMEM((SSD_N, gw), F32)],
        compiler_params=_cparams(("arbitrary", "arbitrary")),
        name=name,
    )(xbc, xbc, xbc, dtg, cumg, h0, dsk)


def _split3(x):
    hi = x.astype(BF16)
    r1 = x - hi.astype(F32)
    mid = r1.astype(BF16)
    lo = (r1 - mid.astype(F32)).astype(BF16)
    return hi, mid, lo


def _norm_router_kernel(x_ref, gain_ref, scale_ref, shift_ref, wr_ref, h_ref, lg_ref):
    h = _rms(x_ref[...], gain_ref[...]) * (1.0 + scale_ref[...]) + shift_ref[...]
    h_ref[...] = h
    lg_ref[...] = jnp.dot(h.astype(BF16), wr_ref[...].astype(BF16), preferred_element_type=F32)


def _norm_router(lay, x, gain, mod3, wr):
    mrow = lambda i: _mod_row(lay, i * RB)
    vspec = lambda fn: pl.BlockSpec((None, 1, D), lambda i: (fn(i), 0, 0))
    return pl.pallas_call(
        _norm_router_kernel,
        out_shape=(jax.ShapeDtypeStruct((lay.m, D), F32), jax.ShapeDtypeStruct((lay.m, LANE), F32)),
        grid=(lay.nb,),
        in_specs=[pl.BlockSpec((RB, D), lambda i: (i, 0)), vspec(lambda i: 0),
                  vspec(lambda i: mrow(i) * 6 + 4), vspec(lambda i: mrow(i) * 6 + 3),
                  pl.BlockSpec((D, LANE), lambda i: (0, 0))],
        out_specs=(pl.BlockSpec((RB, D), lambda i: (i, 0)), pl.BlockSpec((RB, LANE), lambda i: (i, 0))),
        compiler_params=_cparams(("arbitrary",)),
        name="norm_router",
    )(x, gain, mod3, mod3, wr)


GATHER_UNROLL = 8


def _gather_rows(src_hbm, idx_ref, n, dst_row, sem, *, wait, inline=False):
    def one(q):
        cp = pltpu.make_async_copy(src_hbm.at[pl.ds(idx_ref[0, q], 1), :], dst_row(q), sem)
        if wait:
            cp.wait()
        else:
            cp.start()

    if inline:
        for q in range(n):
            one(q)
    else:
        def body(q, carry):
            one(q)
            return carry

        lax.fori_loop(0, n, body, 0, unroll=GATHER_UNROLL)


def _prefetched_gather(src_hbm, cur_idx, next_idx, n, dst, sem):
    step = pl.program_id(0)
    slot = step % 2

    @pl.when(step == 0)
    def _():
        _gather_rows(src_hbm, cur_idx, n, dst(0), sem.at[0], wait=False)

    _gather_rows(src_hbm, cur_idx, n, dst(slot), sem.at[slot], wait=True)
    _gather_rows(src_hbm, next_idx, n, dst(1 - slot), sem.at[1 - slot], wait=False, inline=True)
    return slot


def _drain_gather(src_hbm, next_idx, n, dst, sem):
    step = pl.program_id(0)
    slot = step % 2

    @pl.when(step == pl.num_programs(0) - 1)
    def _():
        _gather_rows(src_hbm, next_idx, n, dst(1 - slot), sem.at[1 - slot], wait=True)


def _expert_kernel(be_ref, rt_ref, rtn_ref, h_hbm, w1_ref, w3_ref, w2_ref, o_ref, xbuf, sem):
    dst = lambda s: (lambda q: xbuf.at[s, pl.ds(q, 1), :])
    slot = _prefetched_gather(h_hbm, rt_ref, rtn_ref, MOE_BLOCK, dst, sem)
    x = xbuf[slot].astype(BF16)
    h1 = jnp.dot(x, w1_ref[...], preferred_element_type=F32)
    h3 = jnp.dot(x, w3_ref[...], preferred_element_type=F32)
    hid = (_silu(h1) * h3).astype(BF16)
    o_ref[...] = jnp.dot(hid, w2_ref[...], preferred_element_type=F32)
    _drain_gather(h_hbm, rtn_ref, MOE_BLOCK, dst, sem)


def _experts(h2, row_tok, w1, w3, w2, block_e):
    n_rows = row_tok.shape[0]
    n_blocks = n_rows // MOE_BLOCK
    rt = row_tok.reshape(n_blocks, 1, MOE_BLOCK)
    wspec = lambda a, b: pl.BlockSpec((None, a, b), lambda i, be: (be[i], 0, 0))
    ispec = lambda f: pl.BlockSpec((None, 1, MOE_BLOCK), lambda i, be: (f(i), 0, 0), memory_space=pltpu.SMEM)
    return pl.pallas_call(
        _expert_kernel,
        out_shape=jax.ShapeDtypeStruct((n_rows, D), F32),
        grid_spec=pltpu.PrefetchScalarGridSpec(
            num_scalar_prefetch=1,
            grid=(n_blocks,),
            in_specs=[ispec(lambda i: i), ispec(lambda i: jnp.minimum(i + 1, n_blocks - 1)),
                      pl.BlockSpec(memory_space=pl.ANY),
                      wspec(D, D_EXP), wspec(D, D_EXP), wspec(D_EXP, D)],
            out_specs=pl.BlockSpec((MOE_BLOCK, D), lambda i, be: (i, 0)),
            scratch_shapes=[pltpu.VMEM((2, MOE_BLOCK, D), F32), pltpu.SemaphoreType.DMA((2,))],
        ),
        compiler_params=_cparams(("arbitrary",)),
        name="experts",
    )(block_e, rt, rt, h2, w1, w3, w2)


def _combine_kernel(dc_ref, dn_ref, x_ref, tw_ref, gate_ref, nf_ref, yb_hbm, *rest, lay, final):
    if final:
        op_ref, os_ref, ybuf, sem = rest
    else:
        o_ref, ybuf, sem = rest
    i = pl.program_id(0)
    n_q = RB * TOP_K
    assert TOP_K == 2
    dst = lambda s: (lambda q: ybuf.at[s, q & 1, pl.ds(q >> 1, 1), :])
    slot = _prefetched_gather(yb_hbm, dc_ref, dn_ref, n_q, dst, sem)
    tw = tw_ref[...]
    y = x_ref[...] + gate_ref[...] * (ybuf[slot, 0] * tw[:, 0:1] + ybuf[slot, 1] * tw[:, 1:2])
    if not final:
        o_ref[...] = y
    else:
        y = _rms(y, nf_ref[...])

        @pl.when(i < lay.nb_ctx)
        def _():
            op_ref[...] = y

        @pl.when(i >= lay.nb_ctx)
        def _():
            os_ref[...] = y

    _drain_gather(yb_hbm, dn_ref, n_q, dst, sem)


def _combine(lay, x, yb, dest, tw, mod3, norm_f, *, final):
    mrow = lambda i: _mod_row(lay, i * RB)
    spec = pl.BlockSpec((RB, D), lambda i: (i, 0))
    n_q = RB * TOP_K
    dq = dest.reshape(lay.nb, 1, n_q)
    ispec = lambda f: pl.BlockSpec((None, 1, n_q), lambda i: (f(i), 0, 0), memory_space=pltpu.SMEM)
    if final:
        out_shape = (jax.ShapeDtypeStruct((lay.m_ctx, D), F32), jax.ShapeDtypeStruct((lay.m - lay.m_ctx, D), F32))
        out_specs = (pl.BlockSpec((RB, D), lambda i: (jnp.minimum(i, lay.nb_ctx - 1), 0)),
                     pl.BlockSpec((RB, D), lambda i: (jnp.maximum(i - lay.nb_ctx, 0), 0)))
    else:
        out_shape, out_specs = jax.ShapeDtypeStruct((lay.m, D), F32), spec
    return pl.pallas_call(
        functools.partial(_combine_kernel, lay=lay, final=final),
        out_shape=out_shape,
        grid=(lay.nb,),
        in_specs=[ispec(lambda i: i), ispec(lambda i: jnp.minimum(i + 1, lay.nb - 1)), spec,
                  pl.BlockSpec((RB, TOP_K), lambda i: (i, 0)),
                  pl.BlockSpec((None, 1, D), lambda i: (mrow(i) * 6 + 5, 0, 0)),
                  pl.BlockSpec((1, D), lambda i: (0, 0)),
                  pl.BlockSpec(memory_space=pl.ANY)],
        out_specs=out_specs,
        scratch_shapes=[pltpu.VMEM((2, TOP_K, RB, D), F32), pltpu.SemaphoreType.DMA((2,))],
        compiler_params=_cparams(("arbitrary",)),
        name="final" if final else "combine",
    )(dq, dq, x, tw, mod3, norm_f, yb)


def _route(logits, router_bias, n_tok):
    n_asg = n_tok * TOP_K
    scores = jax.nn.sigmoid(logits.T)
    biased = scores + router_bias.astype(F32)[:, None]
    bg = biased.reshape(N_GRP, EPG, n_tok)
    pair_sums = [bg[:, a] + bg[:, b] for a in range(EPG) for b in range(a + 1, EPG)]
    group_score = functools.reduce(jnp.maximum, pair_sums)
    best_group = jnp.argmax(group_score, axis=0)
    sel = (jnp.arange(N_GRP)[:, None, None] == best_group[None, None, :])
    vb = jnp.sum(jnp.where(sel, bg, 0.0), axis=0)
    vs = jnp.sum(jnp.where(sel, scores.reshape(N_GRP, EPG, n_tok), 0.0), axis=0)
    loc = jnp.arange(EPG)[:, None]
    i1 = jnp.argmax(vb, axis=0)
    i2 = jnp.argmax(jnp.where(loc == i1[None, :], -jnp.inf, vb), axis=0)
    top_loc = jnp.stack([i1, i2], axis=0)
    top_idx = (best_group[None, :] * EPG + top_loc).astype(jnp.int32)
    top_w = jnp.sum(jnp.where(loc[None, :, :] == top_loc[:, None, :], vs[None, :, :], 0.0), axis=1)
    top_w = top_w / jnp.sum(top_w, axis=0, keepdims=True)
    ex = jnp.arange(N_EXP, dtype=jnp.int32)[None, :, None]
    hit = top_idx[:, None, :] == ex
    cb = MOE_BLOCK
    oh = (hit[0] | hit[1]).astype(F32).reshape(N_EXP, n_tok // cb, cb)
    tri = (jnp.arange(cb)[:, None] < jnp.arange(cb)[None, :]).astype(F32)
    within = jnp.einsum('ebs,st->ebt', oh, tri)
    bsum = oh.sum(-1)
    counts = bsum.sum(-1).astype(jnp.int32)
    padded = (counts + MOE_BLOCK - 1) // MOE_BLOCK * MOE_BLOCK
    padded_end = jnp.cumsum(padded)
    padded_start = padded_end - padded
    base = jnp.cumsum(bsum, axis=1) - bsum + padded_start.astype(F32)[:, None]
    pos_e = (within + base[:, :, None]).reshape(N_EXP, n_tok)
    dest = jnp.sum(jnp.where(hit, pos_e[None], 0.0), axis=1).astype(jnp.int32)
    n_blocks = -(-n_asg // MOE_BLOCK) + N_EXP
    n_rows = n_blocks * MOE_BLOCK
    tok = jnp.broadcast_to(jnp.arange(n_tok, dtype=jnp.int32)[None, :], (TOP_K, n_tok))
    row_tok = jnp.zeros((n_rows,), jnp.int32).at[dest.reshape(-1)].set(tok.reshape(-1))
    block_start = jnp.arange(n_blocks, dtype=jnp.int32) * MOE_BLOCK
    block_e = jnp.minimum(jnp.searchsorted(padded_end, block_start, side='right'), N_EXP - 1).astype(jnp.int32)
    return row_tok, top_w.T, block_e, dest.T


def _sincos_2d(rows, cols, dim):
    quarter = dim // 4
    omega = 1.0 / (POS_BASE ** (jnp.arange(quarter, dtype=F32) / quarter))

    def axis_embed(n):
        ang = jnp.arange(n, dtype=F32)[:, None] * omega[None, :]
        return jnp.concatenate([jnp.sin(ang), jnp.cos(ang)], axis=-1)

    er = jnp.broadcast_to(axis_embed(rows)[:, None, :], (rows, cols, dim // 2))
    ec = jnp.broadcast_to(axis_embed(cols)[None, :, :], (rows, cols, dim // 2))
    return jnp.concatenate([er, ec], axis=-1).reshape(rows * cols, dim)


def _rearranged_w_in(w):
    sizes = (GLA_QK, GLA_QK, GLA_V, GLA_V, 2 * GLA_LR, D_INNER, CONV_DIM, 2 * SSD_HEADS, D, D)
    offs = [0]
    for s in sizes:
        offs.append(offs[-1] + s)
    q, k, v, g, dlr, z, xbc, dt, ga, gb = [w[:, offs[i]:offs[i + 1]] for i in range(len(sizes))]
    zpad = jnp.zeros((w.shape[0], LANE - GLA_LR), w.dtype)
    return jnp.concatenate([q, k, v, g, ga, z, xbc, gb, dt, dlr[:, :GLA_LR], zpad, dlr[:, GLA_LR:], zpad,
                            jnp.zeros((w.shape[0], LANE), w.dtype)], axis=1).astype(BF16)


def _pick_tile(lay, want):
    t = want
    while lay.m_ctx % t or lay.t_lat % t:
        t //= 2
    return t


def kernel(x_prompt, x_sample, c, state_gla, state_ssd, c_ctx, w_ada, b_ada, norm1, norm2, w_in, gla_dw2, gla_db, gla_norm, w_gla_b, conv_w, conv_b, a_log, dt_bias, d_skip, ssd_norm, w_ssd_b, w_out, w_router, router_bias, w_e1, w_e3, w_e2, norm_f):
    n_ctx, t_ctx, _ = x_prompt.shape
    n_lat, t_lat, _ = x_sample.shape
    lay = Layout(n_ctx, t_ctx, n_lat, t_lat)
    depth = w_in.shape[0]
    m = lay.m
    assert t_ctx % RB == 0 and t_lat % RB == 0 and t_lat % GRID_W == 0

    pos = _sincos_2d(t_lat // GRID_W, GRID_W, D)
    x = _embed(lay, x_prompt.reshape(lay.m_ctx, D), x_sample.reshape(n_lat * t_lat, D), pos)

    n_mod = 16
    cond = jnp.concatenate([c_ctx[None, :], c, jnp.zeros((n_mod - 1 - n_lat, D), F32)], axis=0)
    wr = jnp.pad(w_router, ((0, 0), (0, LANE - N_EXP)))

    tm_in = _pick_tile(lay, 2048)
    tm_a = _pick_tile(lay, 512)
    mrow = lambda tm: (lambda i: _mod_row(lay, i * tm))
    new_gla, new_ssd = [], []
    y_prompt = y_sample = None
    for l in range(depth):
        mod = _fused_mm("ada", n_mod, D, 6 * D, tm=n_mod, tn=1024, rows=[(cond, D, 0)], vecs=[],
                        w=w_ada[l].astype(BF16), evecs=[(b_ada[l].reshape(1, 1, 6 * D), lambda i: 0)],
                        prologue=_pro_silu, epilogue=_epi_bias, out_dtype=F32, rc=n_mod)
        mod3 = mod.reshape(n_mod * 6, 1, D)
        n1 = norm1[l].reshape(1, 1, D)
        mr = mrow(tm_in)
        proj = _fused_mm("in_proj", m, D, NP, tm=tm_in, tn=512, rows=[(x, D, 0)],
                         vecs=[(n1, D, lambda i: 0), (mod3, D, lambda i: mr(i) * 6 + 1),
                               (mod3, D, lambda i: mr(i) * 6 + 0)],
                         w=_rearranged_w_in(w_in[l]), prologue=_pro_norm_mod, epilogue=None, out_dtype=F32)

        dw2p = jnp.pad(gla_dw2[l], ((0, 0), (0, LANE - GLA_LR), (0, 0))).astype(BF16)
        db = gla_db[l].reshape(2, 1, GLA_QK)
        o_f, sg_f = _gla_scan(lay, proj, dw2p, db, state_gla, l, reverse=False, name="gla_fwd")
        o_b, sg_b = _gla_scan(lay, proj, dw2p, db, state_gla, l, reverse=True, name="gla_bwd")
        new_gla.append(jnp.stack([sg_f, sg_b], axis=1))
        m_a = _kmm("gla_out", m, GLA_V, D, tm=tm_a, tk=GLA_DV,
                   rows=[(o_f, 0), (o_b, 0), (proj, C_G // GLA_DV)], vecs=[gla_norm[l].reshape(1, GLA_V)],
                   w=w_gla_b[l].astype(BF16), erows=[(proj, C_GA // D)],
                   prologue=_pro_gla_out, epilogue=_epi_gate_a, out_dtype=F32)

        xbc = _conv(lay, proj, conv_w[l], conv_b[l].reshape(1, CONV_DIM))
        dtg, cumg = _dtprep(lay, proj, dt_bias[l].reshape(1, LANE), a_log[l].reshape(1, LANE))
        h0 = state_ssd[:, l].reshape(n_lat, 2, SSD_G, SSD_HPG, SSD_N, SSD_P)
        h0 = h0.transpose(0, 1, 2, 4, 3, 5).reshape(n_lat, 2, SSD_G, SSD_N, SSD_HPG * SSD_P)
        dsk = jnp.repeat(d_skip[l], SSD_P).reshape(1, D_INNER)
        y_f, hs_f = _ssd_scan(lay, xbc, dtg, cumg, h0, dsk, reverse=False, name="ssd_fwd")
        y_b, hs_b = _ssd_scan(lay, xbc, dtg, cumg, h0, dsk, reverse=True, name="ssd_bwd")
        hs = jnp.stack([hs_f, hs_b], axis=1).reshape(n_ctx, 2, SSD_G, SSD_N, SSD_HPG, SSD_P)
        new_ssd.append(hs.transpose(0, 1, 2, 4, 3, 5).reshape(n_ctx, 2, SSD_HEADS, SSD_N, SSD_P))
        merged = _kmm("ssd_out", m, D_INNER, D, tm=tm_a, tk=512,
                      rows=[(y_f, 0), (y_b, 0), (proj, C_Z // 512)], vecs=[ssd_norm[l].reshape(1, D_INNER)],
                      w=w_ssd_b[l].astype(BF16), erows=[(m_a, 0), (proj, C_GB // D)],
                      prologue=_pro_ssd_out, epilogue=_epi_gate_b, out_dtype=BF16)

        x = _fused_mm("mix_out", m, D, D, tm=tm_in, tn=512, rows=[(merged, D, 0)], vecs=[],
                      w=w_out[l].astype(BF16), erows=[(x, 0)], evecs=[(mod3, lambda i: mr(i) * 6 + 2)],
                      prologue=None, epilogue=_epi_resid, out_dtype=F32)

        h2, logits = _norm_router(lay, x, norm2[l].reshape(1, 1, D), mod3, wr)
        row_tok, top_w, block_e, pos_tk = _route(logits[:, :N_EXP], router_bias, m)
        yb = _experts(h2, row_tok, w_e1[l].astype(BF16), w_e3[l].astype(BF16), w_e2[l].astype(BF16), block_e)
        if l + 1 < depth:
            x = _combine(lay, x, yb, pos_tk, top_w, mod3, norm_f.reshape(1, D), final=False)
        else:
            y_prompt, y_sample = _combine(lay, x, yb, pos_tk, top_w, mod3, norm_f.reshape(1, D), final=True)

    return (y_prompt.reshape(n_ctx, t_ctx, D), y_sample.reshape(n_lat, t_lat, D),
            jnp.stack(new_gla, axis=1), jnp.stack(new_ssd, axis=1))
```
